```python
import jax
import jax.numpy as jnp
from jax import lax
import numpy as np


D_MODEL = 2048
BATCH = 4
SEQ = 4096
DEPTH = 1

N_META = 16
CHUNK = 64
META_PAD = CHUNK - N_META
MLSTM_HEADS = 8
QK_HEAD_DIM = D_MODEL // 16
V_HEAD_DIM = D_MODEL // 8
D_QK = MLSTM_HEADS * QK_HEAD_DIM
D_V = MLSTM_HEADS * V_HEAD_DIM
D_CONV = D_MODEL
CONV_WIDTH = 31
PEER_HEADS = 8
PEER_KEYS = 128
PEER_EXPERTS = PEER_KEYS * PEER_KEYS
PEER_TOPK = 16
PEER_QDIM = 256
PEER_HALF = PEER_QDIM // 2
PEER_BLOCK = 128
FORGET_BIAS = 3.0
LOG_ZERO_GATE = -1.0e4
EPS = 1e-6
SEG_SIZES = (D_QK, D_QK, D_V, D_V, MLSTM_HEADS, MLSTM_HEADS, MLSTM_HEADS, MLSTM_HEADS, 2 * D_CONV, 2 * D_MODEL)
SPLIT_POINTS = tuple(sum(SEG_SIZES[:i + 1]) for i in range(len(SEG_SIZES) - 1))
D_IN_TOTAL = sum(SEG_SIZES)

kernel_name = 'hybrid_mlstm_conformer_peer_encoder'


def rmsnorm(x, w):
    xf = x.astype(jnp.float32)
    y = xf * lax.rsqrt(jnp.mean(xf * xf, axis=-1, keepdims=True) + EPS)
    return (y * w.astype(jnp.float32)).astype(x.dtype)


def layernorm(x, w, b):
    xf = x.astype(jnp.float32)
    xc = xf - jnp.mean(xf, axis=-1, keepdims=True)
    var = jnp.mean(xc * xc, axis=-1, keepdims=True)
    y = xc * lax.rsqrt(var + EPS) * w.astype(jnp.float32) + b.astype(jnp.float32)
    return y.astype(x.dtype)


def mlstm_chunkwise(q, k, v, log_i, log_f):
    B, H, P, dk = q.shape
    dv = v.shape[-1]
    n = P // CHUNK
    q = q.reshape(B, H, n, CHUNK, dk) * (dk ** -0.5)
    k = k.reshape(B, H, n, CHUNK, dk)
    v = v.reshape(B, H, n, CHUNK, dv)
    log_i = log_i.reshape(B, H, n, CHUNK)
    log_f = log_f.reshape(B, H, n, CHUNK)
    b = jnp.cumsum(log_f, axis=-1)
    g = b[..., -1]
    a = g[..., None] - b + log_i

    def step(carry, xs):
        C, nv, m = carry
        k_c, v_c, a_c, g_c = xs
        m_new = jnp.maximum(g_c + m, jnp.max(a_c, axis=-1))
        decay = jnp.exp(g_c + m - m_new)
        w = jnp.exp(a_c - m_new[..., None])
        C_new = decay[..., None, None] * C + jnp.einsum('bhs,bhsv,bhsd->bhvd', w, v_c, k_c)
        n_new = decay[..., None] * nv + jnp.einsum('bhs,bhsd->bhd', w, k_c)
        return (C_new, n_new, m_new), (C, nv, m)

    init = (jnp.zeros((B, H, dv, dk), jnp.float32), jnp.zeros((B, H, dk), jnp.float32),
            jnp.zeros((B, H), jnp.float32))
    xs = (jnp.moveaxis(k, 2, 0), jnp.moveaxis(v, 2, 0), jnp.moveaxis(a, 2, 0), jnp.moveaxis(g, 2, 0))
    _, (C_prev, n_prev, m_prev) = lax.scan(step, init, xs)
    C_prev = jnp.moveaxis(C_prev, 0, 2)
    n_prev = jnp.moveaxis(n_prev, 0, 2)
    m_prev = jnp.moveaxis(m_prev, 0, 2)

    inter = b + m_prev[..., None]
    lower = jnp.tril(jnp.ones((CHUNK, CHUNK), dtype=bool))
    intra = b[..., :, None] - b[..., None, :] + log_i[..., None, :]
    intra = jnp.where(lower, intra, -jnp.inf)
    m = jnp.maximum(inter, jnp.max(intra, axis=-1))
    S = jnp.einsum('bhncd,bhnsd->bhncs', q, k) * jnp.exp(intra - m[..., None])
    w_inter = jnp.exp(inter - m)
    num = jnp.einsum('bhncs,bhnsv->bhncv', S, v) + w_inter[..., None] * jnp.einsum('bhnvd,bhncd->bhncv', C_prev, q)
    den = jnp.sum(S, axis=-1) + w_inter * jnp.einsum('bhnd,bhncd->bhnc', n_prev, q)
    h = num / jnp.maximum(jnp.abs(den), jnp.exp(-m))[..., None]
    return h.reshape(B, H, P, dv)


def mlstm_branch(q, k, v, o, i_f, f_f, i_b, f_b, norm_w, w_out):
    B, L, _ = q.shape

    def heads(t, dh):
        t = t.astype(jnp.float32).reshape(B, L, MLSTM_HEADS, dh).transpose(0, 2, 1, 3)
        return jnp.pad(t, ((0, 0), (0, 0), (META_PAD, 0), (0, 0)))

    def gate(t, fill):
        t = t.astype(jnp.float32).transpose(0, 2, 1)
        return jnp.pad(t, ((0, 0), (0, 0), (META_PAD, 0)), constant_values=fill)

    qh = heads(q, QK_HEAD_DIM)
    kh = heads(k, QK_HEAD_DIM)
    vh = heads(v, V_HEAD_DIM)
    li_f = gate(i_f, LOG_ZERO_GATE)
    lf_f = gate(jax.nn.log_sigmoid(f_f.astype(jnp.float32)), 0.0)
    li_b = gate(i_b, LOG_ZERO_GATE)
    lf_b = gate(jax.nn.log_sigmoid(f_b.astype(jnp.float32)), 0.0)

    def flip(t):
        return jnp.flip(t, axis=2)

    h_fwd = mlstm_chunkwise(qh, kh, vh, li_f, lf_f)
    h_bwd = flip(mlstm_chunkwise(flip(qh), flip(kh), flip(vh), flip(li_b), flip(lf_b)))
    hs = (h_fwd + h_bwd)[:, :, META_PAD:]
    hs = hs * lax.rsqrt(jnp.mean(hs * hs, axis=-1, keepdims=True) + EPS)
    hs = hs.transpose(0, 2, 1, 3).reshape(B, L, D_V) * norm_w.astype(jnp.float32)
    hs = hs * jax.nn.sigmoid(o.astype(jnp.float32))
    return hs.astype(q.dtype) @ w_out


def conv_branch(glu, conv_w, conv_b, ln_w, ln_b, w_out, b_out):
    a, g = jnp.split(glu, 2, axis=-1)
    c = a * jax.nn.sigmoid(g)
    c = lax.conv_general_dilated(
        c, conv_w[:, None, :].astype(c.dtype), window_strides=(1,),
        padding=[(CONV_WIDTH // 2, CONV_WIDTH // 2)],
        dimension_numbers=('NWC', 'WIO', 'NWC'), feature_group_count=D_CONV) + conv_b
    c = jax.nn.silu(layernorm(c, ln_w, ln_b))
    return c @ w_out + b_out


def peer(u, w_q, sub_keys, expert_u, expert_v):
    B, L, D = u.shape
    T = B * L
    n_blk = -(-T // PEER_BLOCK)
    t = jnp.pad(u.reshape(T, D), ((0, n_blk * PEER_BLOCK - T), (0, 0))).reshape(n_blk, PEER_BLOCK, D)

    def block(tb):
        qry = (tb @ w_q).reshape(PEER_BLOCK, PEER_HEADS, 2, PEER_HALF)
        s = jnp.einsum('thpd,pkd->thpk', qry, sub_keys)
        sv, si = lax.top_k(s, PEER_TOPK)
        cand = (sv[:, :, 0, :, None] + sv[:, :, 1, None, :]).reshape(PEER_BLOCK, PEER_HEADS, PEER_TOPK * PEER_TOPK)
        cidx = (si[:, :, 0, :, None] * PEER_KEYS + si[:, :, 1, None, :]).reshape(PEER_BLOCK, PEER_HEADS, PEER_TOPK * PEER_TOPK)
        top, pos = lax.top_k(cand, PEER_TOPK)
        eidx = jnp.take_along_axis(cidx, pos, axis=-1)
        gates = jax.nn.softmax(top.astype(jnp.float32), axis=-1)
        z = jnp.einsum('thed,td->the', expert_u[eidx], tb)
        act = (gates * jax.nn.gelu(z.astype(jnp.float32))).astype(tb.dtype)
        return jnp.einsum('the,thed->td', act, expert_v[eidx])

    out = lax.map(block, t)
    return out.reshape(n_blk * PEER_BLOCK, D)[:T].reshape(B, L, D)


def hybrid_layer(h, norm1_w, w_in, b_in, mlstm_norm_w, w_mlstm_out, conv_w, conv_b, conv_ln_w, conv_ln_b,
                 w_conv_out, b_conv_out, w_o, norm2_w, peer_w_q, peer_sub_keys, peer_u, peer_v):
    u = rmsnorm(h, norm1_w)
    proj = u @ w_in + b_in
    q, k, v, o, i_f, f_f, i_b, f_b, glu, gates = jnp.split(proj, SPLIT_POINTS, axis=-1)
    y_a = mlstm_branch(q, k, v, o, i_f, f_f, i_b, f_b, mlstm_norm_w, w_mlstm_out)
    y_b = conv_branch(glu, conv_w, conv_b, conv_ln_w, conv_ln_b, w_conv_out, b_conv_out)
    g_a, g_b = jnp.split(gates, 2, axis=-1)
    merged = jax.nn.sigmoid(g_a) * y_a + jax.nn.sigmoid(g_b) * y_b
    h = h + merged @ w_o
    h = h + peer(rmsnorm(h, norm2_w), peer_w_q, peer_sub_keys, peer_u, peer_v)
    return h


def setup_inputs(seed: int = 0) -> dict:
    key = jax.random.key(seed)
    ks = jax.random.split(key, 20)

    def nrm(kk, shape, scale):
        return jax.random.normal(kk, shape, jnp.float32) * scale

    f_f0 = SPLIT_POINTS[4]
    f_b0 = SPLIT_POINTS[6]
    b_in = nrm(ks[4], (DEPTH, D_IN_TOTAL), 0.02)
    b_in = b_in.at[:, f_f0:f_f0 + MLSTM_HEADS].add(FORGET_BIAS).at[:, f_b0:f_b0 + MLSTM_HEADS].add(FORGET_BIAS)
    return {
        'x': nrm(ks[0], (BATCH, SEQ, D_MODEL), 1.0),
        'meta_tokens': nrm(ks[1], (N_META, D_MODEL), 1.0),
        'norm1_w': 1.0 + nrm(ks[2], (DEPTH, D_MODEL), 0.02),
        'w_in': nrm(ks[3], (DEPTH, D_MODEL, D_IN_TOTAL), D_MODEL ** -0.5),
        'b_in': b_in,
        'mlstm_norm_w': 1.0 + nrm(ks[5], (DEPTH, D_V), 0.02),
        'w_mlstm_out': nrm(ks[6], (DEPTH, D_V, D_MODEL), D_V ** -0.5),
        'conv_w': nrm(ks[7], (DEPTH, CONV_WIDTH, D_CONV), CONV_WIDTH ** -0.5),
        'conv_b': nrm(ks[8], (DEPTH, D_CONV), 0.02),
        'conv_ln_w': 1.0 + nrm(ks[9], (DEPTH, D_CONV), 0.02),
        'conv_ln_b': nrm(ks[10], (DEPTH, D_CONV), 0.02),
        'w_conv_out': nrm(ks[11], (DEPTH, D_CONV, D_MODEL), D_CONV ** -0.5),
        'b_conv_out': nrm(ks[12], (DEPTH, D_MODEL), 0.02),
        'w_o': nrm(ks[13], (DEPTH, D_MODEL, D_MODEL), D_MODEL ** -0.5),
        'norm2_w': 1.0 + nrm(ks[14], (DEPTH, D_MODEL), 0.02),
        'peer_w_q': nrm(ks[15], (DEPTH, D_MODEL, PEER_HEADS * PEER_QDIM), D_MODEL ** -0.5),
        'peer_sub_keys': nrm(ks[16], (DEPTH, 2, PEER_KEYS, PEER_HALF), PEER_HALF ** -0.5),
        'peer_u': nrm(ks[17], (DEPTH, PEER_EXPERTS, D_MODEL), D_MODEL ** -0.5),
        'peer_v': nrm(ks[18], (DEPTH, PEER_EXPERTS, D_MODEL), 0.5),
        'final_norm_w': 1.0 + nrm(ks[19], (D_MODEL,), 0.02),
    }


def reference(x, meta_tokens, norm1_w, w_in, b_in, mlstm_norm_w, w_mlstm_out, conv_w, conv_b, conv_ln_w,
              conv_ln_b, w_conv_out, b_conv_out, w_o, norm2_w, peer_w_q, peer_sub_keys, peer_u, peer_v,
              final_norm_w):
    B = x.shape[0]
    meta = jnp.broadcast_to(meta_tokens[None].astype(x.dtype), (B, N_META, x.shape[-1]))
    h = jnp.concatenate([meta, x], axis=1)
    for layer in range(DEPTH):
        h = hybrid_layer(h, norm1_w[layer], w_in[layer], b_in[layer], mlstm_norm_w[layer], w_mlstm_out[layer],
                         conv_w[layer], conv_b[layer], conv_ln_w[layer], conv_ln_b[layer], w_conv_out[layer],
                         b_conv_out[layer], w_o[layer], norm2_w[layer], peer_w_q[layer], peer_sub_keys[layer],
                         peer_u[layer], peer_v[layer])
    h = rmsnorm(h, final_norm_w)
    return h[:, N_META:]
```

```python
import functools
import math

import jax
import jax.numpy as jnp
from jax import lax
from jax.experimental import pallas as pl
from jax.experimental.pallas import tpu as pltpu

F32 = jnp.float32
BF16 = jnp.bfloat16

LANES = 128
SUBLANES = 8
BF16_ROWS = 16
VMEM_LIMIT_BYTES = 56 * 1024 * 1024

N_META = 16
MLSTM_HEADS = 8
CONV_WIDTH = 31
CONV_HALF = CONV_WIDTH // 2
PEER_HEADS = 8
PEER_TOPK = 16
FORGET_GATE_COLS = 4 * MLSTM_HEADS
LOG_ZERO_GATE = -1.0e4
EPS = 1e-6
NEG_BIG = -1.0e30

ROW_TILE = 512
COL_TILE = 1024
CHUNK = 128
HALO = 16
PEER_TOKEN_TILE = 512
PEER_EXPERT_TILE = 512
OUT_TILE = 128


def _cparams(*sem):
    return pltpu.CompilerParams(dimension_semantics=sem, vmem_limit_bytes=VMEM_LIMIT_BYTES)


def _sigmoid(x):
    return 1.0 / (1.0 + jnp.exp(-x))


def _log_sigmoid(x):
    return jnp.minimum(x, 0.0) - jnp.log(1.0 + jnp.exp(-jnp.abs(x)))


def _gelu_tanh(x):
    c = math.sqrt(2.0 / math.pi)
    return 0.5 * x * (1.0 + jnp.tanh(c * (x + 0.044715 * (x * x * x))))


def _inproj_kernel(h_ref, mask_ref, nw_ref, wm_ref, bm_ref, wg_ref, bg_ref, proj_ref, gates_ref, u_sc):
    @pl.when(pl.program_id(1) == 0)
    def _():
        x = h_ref[...]
        r = lax.rsqrt(jnp.mean(x * x, axis=-1, keepdims=True) + EPS)
        u = (x * r * nw_ref[...]).astype(BF16)
        u_sc[...] = u
        g = jnp.dot(u, wg_ref[...], preferred_element_type=F32) + bg_ref[...]
        gates_ref[...] = g * mask_ref[...]

    acc = jnp.dot(u_sc[...], wm_ref[...], preferred_element_type=F32) + bm_ref[...]
    proj_ref[...] = (acc * mask_ref[...]).astype(BF16)


def _inproj(hp, mask, nw, wm, bm, wg, bg):
    rows, d = hp.shape
    n_main = wm.shape[1]
    grid = (rows // ROW_TILE, n_main // COL_TILE)
    return pl.pallas_call(
        _inproj_kernel,
        grid=grid,
        in_specs=[
            pl.BlockSpec((ROW_TILE, d), lambda i, j: (i, 0)),
            pl.BlockSpec((ROW_TILE, 1), lambda i, j: (i, 0)),
            pl.BlockSpec((1, d), lambda i, j: (0, 0)),
            pl.BlockSpec((d, COL_TILE), lambda i, j: (0, j)),
            pl.BlockSpec((1, COL_TILE), lambda i, j: (0, j)),
            pl.BlockSpec((d, LANES), lambda i, j: (0, 0)),
            pl.BlockSpec((1, LANES), lambda i, j: (0, 0)),
        ],
        out_specs=[
            pl.BlockSpec((ROW_TILE, COL_TILE), lambda i, j: (i, j)),
            pl.BlockSpec((ROW_TILE, LANES), lambda i, j: (i, 0)),
        ],
        out_shape=[
            jax.ShapeDtypeStruct((rows, n_main), BF16),
            jax.ShapeDtypeStruct((rows, LANES), F32),
        ],
        scratch_shapes=[pltpu.VMEM((ROW_TILE, d), BF16)],
        compiler_params=_cparams("parallel", "arbitrary"),
        name="inproj",
    )(hp, mask, nw, wm, bm, wg, bg)


def _mlstm_kernel(pad, n_chunks, dk, dv,
                  qf_ref, kf_ref, vf_ref, gf_ref, gtf_ref,
                  qb_ref, kb_ref, vb_ref, gb_ref, gtb_ref,
                  hf_ref, hb_ref, c_sc, n_sc, m_sc):
    c = pl.program_id(1)

    @pl.when(c == 0)
    def _():
        c_sc[...] = jnp.zeros_like(c_sc)
        n_sc[...] = jnp.zeros_like(n_sc)
        m_sc[...] = jnp.zeros_like(m_sc)

    scale = dk ** -0.5
    row = lax.broadcasted_iota(jnp.int32, (CHUNK, CHUNK), 0)
    col = lax.broadcasted_iota(jnp.int32, (CHUNK, CHUNK), 1)
    lower = col <= row
    upper = col >= row
    row1 = lax.broadcasted_iota(jnp.int32, (CHUNK, 1), 0)
    col1 = lax.broadcasted_iota(jnp.int32, (1, CHUNK), 1)

    dirs = (
        (qf_ref, kf_ref, vf_ref, gf_ref, gtf_ref, hf_ref, c, lower, upper),
        (qb_ref, kb_ref, vb_ref, gb_ref, gtb_ref, hb_ref, n_chunks - 1 - c, upper, lower),
    )
    for d, (q_ref, k_ref, v_ref, g_ref, gt_ref, out_ref, chunk, mask, mask_t) in enumerate(dirs):
        base = chunk * CHUNK
        valid_c = (base + row1) >= pad
        valid_r = (base + col1) >= pad
        g0 = 2 * MLSTM_HEADS * d
        gc = g_ref[...]
        gr = gt_ref[...]
        li_c_all = jnp.where(valid_c, gc[:, g0:g0 + MLSTM_HEADS], LOG_ZERO_GATE)
        lf_c_all = jnp.where(valid_c, _log_sigmoid(gc[:, g0 + MLSTM_HEADS:g0 + 2 * MLSTM_HEADS]), 0.0)
        li_r_all = jnp.where(valid_r, gr[g0:g0 + MLSTM_HEADS, :], LOG_ZERO_GATE)
        lf_r_all = jnp.where(valid_r, _log_sigmoid(gr[g0 + MLSTM_HEADS:g0 + 2 * MLSTM_HEADS, :]), 0.0)
        b_c_all = jnp.dot(mask.astype(F32), lf_c_all, precision=lax.Precision.HIGHEST,
                          preferred_element_type=F32)
        b_r_all = jnp.dot(lf_r_all, mask_t.astype(F32), precision=lax.Precision.HIGHEST,
                          preferred_element_type=F32)
        g_all = jnp.sum(lf_c_all, axis=0, keepdims=True)

        for h in range(MLSTM_HEADS):
            idx = d * MLSTM_HEADS + h
            li_c = li_c_all[:, h:h + 1]
            b_c = b_c_all[:, h:h + 1]
            li_r = li_r_all[h:h + 1, :]
            b_r = b_r_all[h:h + 1, :]
            g = g_all[:, h:h + 1]
            m_prev = m_sc[idx, 0:1, 0:1]
            n_prev = n_sc[idx, 0:1, :]
            ct = c_sc[idx]

            qh = q_ref[:, h * dk:(h + 1) * dk]
            kh = k_ref[:, h * dk:(h + 1) * dk]
            vh = v_ref[:, h * dv:(h + 1) * dv]
            k32 = kh.astype(F32)

            a_c = g - b_c + li_c
            m_new = jnp.maximum(g + m_prev, jnp.max(a_c, axis=0, keepdims=True))
            decay = jnp.exp(g + m_prev - m_new)
            w_c = jnp.exp(a_c - m_new)

            inter = b_c + m_prev
            logd = jnp.where(mask, b_c - b_r + li_r, NEG_BIG)
            m_c = jnp.maximum(inter, jnp.max(logd, axis=1, keepdims=True))
            p = jnp.exp(logd - m_c)
            qk = lax.dot_general(qh, kh, (((1,), (1,)), ((), ())), preferred_element_type=F32)
            s = qk * (p * scale)
            w_inter = jnp.exp(inter - m_c) * scale
            qc = jnp.dot(qh, ct.astype(BF16), preferred_element_type=F32)
            num = jnp.dot(s.astype(BF16), vh, preferred_element_type=F32) + w_inter * qc
            qn = jnp.sum(qh.astype(F32) * n_prev, axis=1, keepdims=True)
            den = jnp.sum(s, axis=1, keepdims=True) + w_inter * qn
            out_ref[:, h * dv:(h + 1) * dv] = num / jnp.maximum(jnp.abs(den), jnp.exp(-m_c))

            wv = (w_c * vh.astype(F32)).astype(BF16)
            kt = k32.T.astype(BF16)
            c_sc[idx] = decay * ct + jnp.dot(kt, wv, preferred_element_type=F32)
            n_new = decay * n_prev + jnp.sum(w_c * k32, axis=0, keepdims=True)
            n_sc[idx] = jnp.broadcast_to(n_new, (SUBLANES, dk))
            m_sc[idx] = jnp.broadcast_to(m_new, (SUBLANES, LANES))


def _mlstm(proj, gates, gates_t, batch, lp, pad, dk, dv):
    rows = proj.shape[0]
    n_chunks = lp // CHUNK
    d_qk = MLSTM_HEADS * dk
    d_v = MLSTM_HEADS * dv
    v_blk = (2 * d_qk) // d_v

    def fwd(b, c):
        return b * n_chunks + c

    def bwd(b, c):
        return b * n_chunks + n_chunks - 1 - c

    def specs(pos):
        return [
            pl.BlockSpec((CHUNK, d_qk), lambda b, c: (pos(b, c), 0)),
            pl.BlockSpec((CHUNK, d_qk), lambda b, c: (pos(b, c), 1)),
            pl.BlockSpec((CHUNK, d_v), lambda b, c: (pos(b, c), v_blk)),
            pl.BlockSpec((CHUNK, LANES), lambda b, c: (pos(b, c), 0)),
            pl.BlockSpec((FORGET_GATE_COLS, CHUNK), lambda b, c: (0, pos(b, c))),
        ]

    kern = functools.partial(_mlstm_kernel, pad, n_chunks, dk, dv)
    n_state = 2 * MLSTM_HEADS
    return pl.pallas_call(
        kern,
        grid=(batch, n_chunks),
        in_specs=specs(fwd) + specs(bwd),
        out_specs=[
            pl.BlockSpec((CHUNK, d_v), lambda b, c: (fwd(b, c), 0)),
            pl.BlockSpec((CHUNK, d_v), lambda b, c: (bwd(b, c), 0)),
        ],
        out_shape=[jax.ShapeDtypeStruct((rows, d_v), F32)] * 2,
        scratch_shapes=[
            pltpu.VMEM((n_state, dk, dv), F32),
            pltpu.VMEM((n_state, SUBLANES, dk), F32),
            pltpu.VMEM((n_state, SUBLANES, LANES), F32),
        ],
        compiler_params=_cparams("parallel", "arbitrary"),
        name="mlstm",
    )(proj, proj, proj, gates, gates_t, proj, proj, proj, gates, gates_t)


def _mlstm_out_kernel(dv, hf_ref, hb_ref, o_ref, ga_ref, nw_ref, w_ref, out_ref, y_sc):
    for h in range(MLSTM_HEADS):
        sl = slice(h * dv, (h + 1) * dv)
        hs = hf_ref[:, sl] + hb_ref[:, sl]
        r = lax.rsqrt(jnp.mean(hs * hs, axis=-1, keepdims=True) + EPS)
        y = hs * r * nw_ref[:, sl] * _sigmoid(o_ref[:, sl].astype(F32))
        y_sc[:, sl] = y.astype(BF16)
    ya = jnp.dot(y_sc[...], w_ref[...], preferred_element_type=F32)
    out_ref[...] = _sigmoid(ga_ref[...].astype(F32)) * ya


def _mlstm_out(hf, hb, proj, nw, w, dv, o_blk, ga_blk):
    rows, d_v = hf.shape
    d = w.shape[1]
    return pl.pallas_call(
        functools.partial(_mlstm_out_kernel, dv),
        grid=(rows // ROW_TILE,),
        in_specs=[
            pl.BlockSpec((ROW_TILE, d_v), lambda i: (i, 0)),
            pl.BlockSpec((ROW_TILE, d_v), lambda i: (i, 0)),
            pl.BlockSpec((ROW_TILE, d_v), lambda i: (i, o_blk)),
            pl.BlockSpec((ROW_TILE, d), lambda i: (i, ga_blk)),
            pl.BlockSpec((1, d_v), lambda i: (0, 0)),
            pl.BlockSpec((d_v, d), lambda i: (0, 0)),
        ],
        out_specs=pl.BlockSpec((ROW_TILE, d), lambda i: (i, 0)),
        out_shape=jax.ShapeDtypeStruct((rows, d), F32),
        scratch_shapes=[pltpu.VMEM((ROW_TILE, d_v), BF16)],
        compiler_params=_cparams("parallel"),
        name="mlstm_out",
    )(hf, hb, proj, proj, nw, w)


CONV_ROW_BLOCK = 128
CONV_COL_BLOCK = 256


def _conv_kernel(a_ref, g_ref, ap_ref, gp_ref, an_ref, gn_ref, cw_ref, cb_ref, lnw_ref, lnb_ref,
                 w_ref, bo_ref, gb_ref, ya_ref, out_ref, c_sc, conv_sc):
    i = pl.program_id(0)
    last = pl.num_programs(0) - 1

    def glu(a, g):
        return a.astype(F32) * _sigmoid(g.astype(F32))

    c_sc[HALO:HALO + ROW_TILE, :] = glu(a_ref[...], g_ref[...])
    c_sc[0:HALO, :] = jnp.where(i > 0, glu(ap_ref[...], gp_ref[...]), 0.0)
    c_sc[HALO + ROW_TILE:, :] = jnp.where(i < last, glu(an_ref[...], gn_ref[...]), 0.0)

    d = a_ref.shape[1]
    first = HALO - CONV_HALF
    span = CONV_ROW_BLOCK + 3 * SUBLANES
    for rb in range(ROW_TILE // CONV_ROW_BLOCK):
        r0 = rb * CONV_ROW_BLOCK
        for cb in range(d // CONV_COL_BLOCK):
            cs = slice(cb * CONV_COL_BLOCK, (cb + 1) * CONV_COL_BLOCK)
            acc = jnp.broadcast_to(cb_ref[:, cs], (CONV_ROW_BLOCK, CONV_COL_BLOCK))
            for sh in range(SUBLANES):
                xs = c_sc[r0 + sh:r0 + sh + span, cs]
                for al in range(span // SUBLANES - CONV_ROW_BLOCK // SUBLANES + 1):
                    tap = al * SUBLANES + sh - first
                    if 0 <= tap < CONV_WIDTH:
                        acc = acc + cw_ref[tap:tap + 1, cs] * xs[al * SUBLANES:al * SUBLANES + CONV_ROW_BLOCK]
            conv_sc[r0:r0 + CONV_ROW_BLOCK, cs] = acc

    x = conv_sc[...]
    mu = jnp.mean(x, axis=-1, keepdims=True)
    xc = x - mu
    var = jnp.mean(xc * xc, axis=-1, keepdims=True)
    y = xc * lax.rsqrt(var + EPS) * lnw_ref[...] + lnb_ref[...]
    y = y * _sigmoid(y)
    yb = jnp.dot(y.astype(BF16), w_ref[...], preferred_element_type=F32) + bo_ref[...]
    out_ref[...] = ya_ref[...] + _sigmoid(gb_ref[...].astype(F32)) * yb


def _conv_merge(proj, ya, cw, cb, lnw, lnb, w, bo, a_blk, g_blk, gb_blk):
    rows = proj.shape[0]
    d = w.shape[0]
    n_tiles = rows // ROW_TILE
    per = ROW_TILE // HALO
    n_halo = rows // HALO
    a_col = a_blk
    g_col = g_blk

    def prev(i):
        return jnp.maximum(i * per - 1, 0)

    def nxt(i):
        return jnp.minimum((i + 1) * per, n_halo - 1)

    return pl.pallas_call(
        _conv_kernel,
        grid=(n_tiles,),
        in_specs=[
            pl.BlockSpec((ROW_TILE, d), lambda i: (i, a_col)),
            pl.BlockSpec((ROW_TILE, d), lambda i: (i, g_col)),
            pl.BlockSpec((HALO, d), lambda i: (prev(i), a_col)),
            pl.BlockSpec((HALO, d), lambda i: (prev(i), g_col)),
            pl.BlockSpec((HALO, d), lambda i: (nxt(i), a_col)),
            pl.BlockSpec((HALO, d), lambda i: (nxt(i), g_col)),
            pl.BlockSpec((CONV_WIDTH, d), lambda i: (0, 0)),
            pl.BlockSpec((1, d), lambda i: (0, 0)),
            pl.BlockSpec((1, d), lambda i: (0, 0)),
            pl.BlockSpec((1, d), lambda i: (0, 0)),
            pl.BlockSpec((d, d), lambda i: (0, 0)),
            pl.BlockSpec((1, d), lambda i: (0, 0)),
            pl.BlockSpec((ROW_TILE, d), lambda i: (i, gb_blk)),
            pl.BlockSpec((ROW_TILE, d), lambda i: (i, 0)),
        ],
        out_specs=pl.BlockSpec((ROW_TILE, d), lambda i: (i, 0)),
        out_shape=jax.ShapeDtypeStruct((rows, d), F32),
        scratch_shapes=[
            pltpu.VMEM((ROW_TILE + 2 * HALO, d), F32),
            pltpu.VMEM((ROW_TILE, d), F32),
        ],
        compiler_params=_cparams("parallel"),
        name="conv_merge",
    )(proj, proj, proj, proj, proj, proj, cw, cb, lnw, lnb, w, bo, proj, ya)


def _wo_kernel(m_ref, h_ref, w_ref, nw_ref, h2_ref, ut_ref):
    h2 = h_ref[...] + jnp.dot(m_ref[...].astype(BF16), w_ref[...], preferred_element_type=F32)
    h2_ref[...] = h2
    r = lax.rsqrt(jnp.mean(h2 * h2, axis=-1, keepdims=True) + EPS)
    u = h2 * r * nw_ref[...]
    ut_ref[...] = u.T.astype(BF16)


def _wo_residual(merged, hp, w, nw):
    rows, d = hp.shape
    return pl.pallas_call(
        _wo_kernel,
        grid=(rows // ROW_TILE,),
        in_specs=[
            pl.BlockSpec((ROW_TILE, d), lambda i: (i, 0)),
            pl.BlockSpec((ROW_TILE, d), lambda i: (i, 0)),
            pl.BlockSpec((d, d), lambda i: (0, 0)),
            pl.BlockSpec((1, d), lambda i: (0, 0)),
        ],
        out_specs=[
            pl.BlockSpec((ROW_TILE, d), lambda i: (i, 0)),
            pl.BlockSpec((d, ROW_TILE), lambda i: (0, i)),
        ],
        out_shape=[
            jax.ShapeDtypeStruct((rows, d), F32),
            jax.ShapeDtypeStruct((d, rows), BF16),
        ],
        compiler_params=_cparams("parallel"),
        name="wo_residual",
    )(merged, hp, w, nw)


def _top_values(x, k):
    vals = []
    for _ in range(k):
        mx = jnp.max(x, axis=0, keepdims=True)
        vals.append(mx)
        x = jnp.where(x == mx, -jnp.inf, x)
    return jnp.concatenate(vals, axis=0)


def _peer_route_kernel(n_keys, ut_ref, wq_ref, keys_ref, s1_ref, e1_ref, s2_ref, e2_ref, th_ref):
    ut = ut_ref[...]
    for h in range(PEER_HEADS):
        tops = []
        scores = []
        for p in range(2):
            r0 = (2 * h + p) * n_keys
            q = jnp.dot(wq_ref[r0:r0 + n_keys, :], ut, preferred_element_type=F32)
            s = jnp.dot(keys_ref[p], q.astype(BF16), preferred_element_type=F32)
            scores.append(s)
            tops.append(_top_values(s, PEER_TOPK))
        a1, a2 = tops
        cand = jnp.concatenate([a1[x:x + 1, :] + a2 for x in range(PEER_TOPK)], axis=0)
        top = _top_values(cand, PEER_TOPK)
        cmax = top[0:1, :]
        z = jnp.sum(jnp.exp(top - cmax), axis=0, keepdims=True)
        s1_ref[h] = scores[0]
        s2_ref[h] = scores[1]
        e1_ref[h] = jnp.exp(scores[0] - a1[0:1, :]) / z
        e2_ref[h] = jnp.exp(scores[1] - a2[0:1, :])
        th_ref[h:h + 1, :] = top[PEER_TOPK - 1:PEER_TOPK, :]


def _peer_route(ut, wq_t, keys):
    d, rows = ut.shape
    n_keys = keys.shape[1]
    tm = PEER_TOKEN_TILE
    big = pl.BlockSpec((PEER_HEADS, n_keys, tm), lambda t: (0, 0, t))
    big_shape = jax.ShapeDtypeStruct((PEER_HEADS, n_keys, rows), F32)
    return pl.pallas_call(
        functools.partial(_peer_route_kernel, n_keys),
        grid=(rows // tm,),
        in_specs=[
            pl.BlockSpec((d, tm), lambda t: (0, t)),
            pl.BlockSpec(wq_t.shape, lambda t: (0, 0)),
            pl.BlockSpec(keys.shape, lambda t: (0, 0, 0)),
        ],
        out_specs=[big, big, big, big, pl.BlockSpec((PEER_HEADS, tm), lambda t: (0, t))],
        out_shape=[big_shape, big_shape, big_shape, big_shape,
                   jax.ShapeDtypeStruct((PEER_HEADS, rows), F32)],
        compiler_params=_cparams("parallel"),
        name="peer_route",
    )(ut, wq_t, keys)


def _peer_main_kernel(n_keys, ut_ref, u_ref, vt_ref, s1_ref, e1_ref, s2_ref, e2_ref, th_ref,
                      out_ref, act_sc):
    e = pl.program_id(1)

    @pl.when(e == 0)
    def _():
        out_ref[...] = jnp.zeros_like(out_ref)

    z = jnp.dot(u_ref[...], ut_ref[...], preferred_element_type=F32)
    per = PEER_EXPERT_TILE // n_keys
    for il in range(per):
        i = e * per + il
        gate = None
        for h in range(PEER_HEADS):
            c = s2_ref[h] + s1_ref[h, pl.ds(i, 1), :]
            w = e2_ref[h] * e1_ref[h, pl.ds(i, 1), :]
            term = jnp.where(c >= th_ref[h:h + 1, :], w, 0.0)
            gate = term if gate is None else gate + term
        rows = slice(il * n_keys, (il + 1) * n_keys)
        act_sc[rows, :] = (gate * _gelu_tanh(z[rows, :])).astype(BF16)
    out_ref[...] += jnp.dot(vt_ref[...], act_sc[...], preferred_element_type=F32)


def _peer_main(ut, u_w, v_t, s1, e1, s2, e2, th):
    d, rows = ut.shape
    n_exp = u_w.shape[0]
    n_keys = s1.shape[1]
    tm, te = PEER_TOKEN_TILE, PEER_EXPERT_TILE
    big = pl.BlockSpec((PEER_HEADS, n_keys, tm), lambda t, e: (0, 0, t))
    return pl.pallas_call(
        functools.partial(_peer_main_kernel, n_keys),
        grid=(rows // tm, n_exp // te),
        in_specs=[
            pl.BlockSpec((d, tm), lambda t, e: (0, t)),
            pl.BlockSpec((te, d), lambda t, e: (e, 0)),
            pl.BlockSpec((d, te), lambda t, e: (0, e)),
            big, big, big, big,
            pl.BlockSpec((PEER_HEADS, tm), lambda t, e: (0, t)),
        ],
        out_specs=pl.BlockSpec((d, tm), lambda t, e: (0, t)),
        out_shape=jax.ShapeDtypeStruct((d, rows), F32),
        scratch_shapes=[pltpu.VMEM((te, tm), BF16)],
        compiler_params=_cparams("parallel", "arbitrary"),
        name="peer_main",
    )(ut, u_w, v_t, s1, e1, s2, e2, th)


def _final_kernel(pt_ref, h2_ref, nw_ref, out_ref):
    h3 = h2_ref[...] + pt_ref[...].T
    r = lax.rsqrt(jnp.mean(h3 * h3, axis=-1, keepdims=True) + EPS)
    out_ref[0] = h3 * r * nw_ref[...]


def _final(peer_t, h2, nw, batch, seq, lp):
    d = h2.shape[1]
    tiles_per_batch = lp // OUT_TILE
    skip = tiles_per_batch - seq // OUT_TILE

    def pos(b, i):
        return b * tiles_per_batch + skip + i

    return pl.pallas_call(
        _final_kernel,
        grid=(batch, seq // OUT_TILE),
        in_specs=[
            pl.BlockSpec((d, OUT_TILE), lambda b, i: (0, pos(b, i))),
            pl.BlockSpec((OUT_TILE, d), lambda b, i: (pos(b, i), 0)),
            pl.BlockSpec((1, d), lambda b, i: (0, 0)),
        ],
        out_specs=pl.BlockSpec((1, OUT_TILE, d), lambda b, i: (b, i, 0)),
        out_shape=jax.ShapeDtypeStruct((batch, seq, d), F32),
        compiler_params=_cparams("parallel", "parallel"),
        name="final_norm",
    )(peer_t, h2, nw)


def _layer(hp, mask, batch, lp, pad, norm1_w, w_in, b_in, mlstm_norm_w, w_mlstm_out, conv_w, conv_b,
           conv_ln_w, conv_ln_b, w_conv_out, b_conv_out, w_o, norm2_w, peer_w_q, peer_sub_keys,
           peer_u, peer_v):
    d = hp.shape[1]
    d_v = w_mlstm_out.shape[0]
    dv = d_v // MLSTM_HEADS
    d_conv = w_conv_out.shape[0]
    d_qk = (w_in.shape[1] - 2 * d_v - FORGET_GATE_COLS - 2 * d_conv - 2 * d) // 2
    dk = d_qk // MLSTM_HEADS
    assert d_v == d and d_conv == d and 2 * d_qk == d_v, "column-block indexing assumes these widths"

    g0 = 2 * d_qk + 2 * d_v
    g1 = g0 + FORGET_GATE_COLS
    wm = jnp.concatenate([w_in[:, :g0], w_in[:, g1:]], axis=1).astype(BF16)
    bm = jnp.concatenate([b_in[:g0], b_in[g1:]])[None, :]
    wg = jnp.pad(w_in[:, g0:g1], ((0, 0), (0, LANES - FORGET_GATE_COLS))).astype(BF16)
    bg = jnp.pad(b_in[g0:g1], (0, LANES - FORGET_GATE_COLS))[None, :]
    o_blk, a_blk, g_blk, ga_blk, gb_blk = 2, 3, 4, 5, 6

    proj, gates = _inproj(hp, mask, norm1_w[None, :], wm, bm, wg, bg)
    gates_t = gates[:, :FORGET_GATE_COLS].T
    hf, hb = _mlstm(proj, gates, gates_t, batch, lp, pad, dk, dv)
    ya = _mlstm_out(hf, hb, proj, mlstm_norm_w[None, :], w_mlstm_out.astype(BF16), dv, o_blk, ga_blk)
    merged = _conv_merge(proj, ya, conv_w, conv_b[None, :], conv_ln_w[None, :], conv_ln_b[None, :],
                         w_conv_out.astype(BF16), b_conv_out[None, :], a_blk, g_blk, gb_blk)
    h2, ut = _wo_residual(merged, hp, w_o.astype(BF16), norm2_w[None, :])
    s1, e1, s2, e2, th = _peer_route(ut, peer_w_q.T.astype(BF16), peer_sub_keys.astype(BF16))
    peer_t = _peer_main(ut, peer_u.astype(BF16), peer_v.T.astype(BF16), s1, e1, s2, e2, th)
    return h2, peer_t


def kernel(x, meta_tokens, norm1_w, w_in, b_in, mlstm_norm_w, w_mlstm_out, conv_w, conv_b, conv_ln_w,
           conv_ln_b, w_conv_out, b_conv_out, w_o, norm2_w, peer_w_q, peer_sub_keys, peer_u, peer_v,
           final_norm_w):
    batch, seq, d = x.shape
    depth = norm1_w.shape[0]
    assert depth == 1, "the fused final norm assumes a single layer"
    assert seq % OUT_TILE == 0
    seq_len = N_META + seq
    pad = HALO + (-(seq_len + HALO)) % CHUNK
    lp = pad + seq_len
    rows = batch * lp
    assert rows % ROW_TILE == 0 and rows % PEER_TOKEN_TILE == 0

    meta = jnp.broadcast_to(meta_tokens[None].astype(x.dtype), (batch, N_META, d))
    hp = jnp.concatenate([jnp.zeros((batch, pad, d), x.dtype), meta, x], axis=1).reshape(rows, d)
    mask = jnp.tile((jnp.arange(lp) >= pad).astype(F32), batch)[:, None]

    h2, peer_t = _layer(hp, mask, batch, lp, pad, norm1_w[0], w_in[0], b_in[0], mlstm_norm_w[0],
                        w_mlstm_out[0], conv_w[0], conv_b[0], conv_ln_w[0], conv_ln_b[0],
                        w_conv_out[0], b_conv_out[0], w_o[0], norm2_w[0], peer_w_q[0],
                        peer_sub_keys[0], peer_u[0], peer_v[0])
    return _final(peer_t, h2, final_norm_w[None, :], batch, seq, lp)
```

```python
import functools
import math

import jax
import jax.numpy as jnp
from jax import lax
from jax.experimental import pallas as pl
from jax.experimental.pallas import tpu as pltpu

F32 = jnp.float32
BF16 = jnp.bfloat16

LANES = 128
SUBLANES = 8
BF16_ROWS = 16
VMEM_LIMIT_BYTES = 56 * 1024 * 1024

N_META = 16
MLSTM_HEADS = 8
CONV_WIDTH = 31
CONV_HALF = CONV_WIDTH // 2
PEER_HEADS = 8
PEER_TOPK = 16
FORGET_GATE_COLS = 4 * MLSTM_HEADS
LOG_ZERO_GATE = -1.0e4
EPS = 1e-6
NEG_BIG = -1.0e30

ROW_TILE = 512
COL_TILE = 2048
CHUNK = 128
HALO = 16
PEER_TOKEN_TILE = 512
PEER_EXPERT_TILE = 512
OUT_TILE = 128


def _cparams(*sem):
    return pltpu.CompilerParams(dimension_semantics=sem, vmem_limit_bytes=VMEM_LIMIT_BYTES)


def _sigmoid(x):
    return 1.0 / (1.0 + jnp.exp(-x))


def _log_sigmoid(x):
    return jnp.minimum(x, 0.0) - jnp.log(1.0 + jnp.exp(-jnp.abs(x)))


def _gelu_tanh(x, half=0.5):
    c = math.sqrt(2.0 / math.pi)
    return half * x * (1.0 + jnp.tanh(c * (x + 0.044715 * (x * x * x))))


def _inproj_kernel(h_ref, mask_ref, nw_ref, wm_ref, bm_ref, wg_ref, bg_ref, proj_ref, gates_ref, u_sc):
    @pl.when(pl.program_id(1) == 0)
    def _():
        x = h_ref[...]
        r = lax.rsqrt(jnp.mean(x * x, axis=-1, keepdims=True) + EPS)
        u = (x * r * nw_ref[...]).astype(BF16)
        u_sc[...] = u
        g = jnp.dot(u, wg_ref[...], preferred_element_type=F32) + bg_ref[...]
        gates_ref[...] = g * mask_ref[...]

    acc = jnp.dot(u_sc[...], wm_ref[...], preferred_element_type=F32) + bm_ref[...]
    proj_ref[...] = (acc * mask_ref[...]).astype(BF16)


def _inproj(hp, mask, nw, wm, bm, wg, bg):
    rows, d = hp.shape
    n_main = wm.shape[1]
    grid = (rows // ROW_TILE, n_main // COL_TILE)
    return pl.pallas_call(
        _inproj_kernel,
        grid=grid,
        in_specs=[
            pl.BlockSpec((ROW_TILE, d), lambda i, j: (i, 0)),
            pl.BlockSpec((ROW_TILE, 1), lambda i, j: (i, 0)),
            pl.BlockSpec((1, d), lambda i, j: (0, 0)),
            pl.BlockSpec((d, COL_TILE), lambda i, j: (0, j)),
            pl.BlockSpec((1, COL_TILE), lambda i, j: (0, j)),
            pl.BlockSpec((d, LANES), lambda i, j: (0, 0)),
            pl.BlockSpec((1, LANES), lambda i, j: (0, 0)),
        ],
        out_specs=[
            pl.BlockSpec((ROW_TILE, COL_TILE), lambda i, j: (i, j)),
            pl.BlockSpec((ROW_TILE, LANES), lambda i, j: (i, 0)),
        ],
        out_shape=[
            jax.ShapeDtypeStruct((rows, n_main), BF16),
            jax.ShapeDtypeStruct((rows, LANES), F32),
        ],
        scratch_shapes=[pltpu.VMEM((ROW_TILE, d), BF16)],
        compiler_params=_cparams("parallel", "arbitrary"),
        name="inproj",
    )(hp, mask, nw, wm, bm, wg, bg)


def _mlstm_kernel(pad, n_chunks, dk, dv,
                  qf_ref, kf_ref, vf_ref, gf_ref, gtf_ref,
                  qb_ref, kb_ref, vb_ref, gb_ref, gtb_ref,
                  hf_ref, hb_ref, c_sc, n_sc, m_sc):
    c = pl.program_id(1)

    @pl.when(c == 0)
    def _():
        c_sc[...] = jnp.zeros_like(c_sc)
        n_sc[...] = jnp.zeros_like(n_sc)
        m_sc[...] = jnp.zeros_like(m_sc)

    scale = dk ** -0.5
    row = lax.broadcasted_iota(jnp.int32, (CHUNK, CHUNK), 0)
    col = lax.broadcasted_iota(jnp.int32, (CHUNK, CHUNK), 1)
    lower = col <= row
    upper = col >= row
    row1 = lax.broadcasted_iota(jnp.int32, (CHUNK, 1), 0)
    col1 = lax.broadcasted_iota(jnp.int32, (1, CHUNK), 1)

    dirs = (
        (qf_ref, kf_ref, vf_ref, gf_ref, gtf_ref, hf_ref, c, lower, upper),
        (qb_ref, kb_ref, vb_ref, gb_ref, gtb_ref, hb_ref, n_chunks - 1 - c, upper, lower),
    )
    for d, (q_ref, k_ref, v_ref, g_ref, gt_ref, out_ref, chunk, mask, mask_t) in enumerate(dirs):
        base = chunk * CHUNK
        valid_c = (base + row1) >= pad
        valid_r = (base + col1) >= pad
        g0 = 2 * MLSTM_HEADS * d
        gc = g_ref[...]
        gr = gt_ref[...]
        li_c_all = jnp.where(valid_c, gc[:, g0:g0 + MLSTM_HEADS], LOG_ZERO_GATE)
        lf_c_all = jnp.where(valid_c, _log_sigmoid(gc[:, g0 + MLSTM_HEADS:g0 + 2 * MLSTM_HEADS]), 0.0)
        li_r_all = jnp.where(valid_r, gr[g0:g0 + MLSTM_HEADS, :], LOG_ZERO_GATE)
        lf_r_all = jnp.where(valid_r, _log_sigmoid(gr[g0 + MLSTM_HEADS:g0 + 2 * MLSTM_HEADS, :]), 0.0)
        b_c_all = jnp.dot(mask.astype(F32), lf_c_all, precision=lax.Precision.HIGHEST,
                          preferred_element_type=F32)
        b_r_all = jnp.dot(lf_r_all, mask_t.astype(F32), precision=lax.Precision.HIGHEST,
                          preferred_element_type=F32)
        g_all = jnp.sum(lf_c_all, axis=0, keepdims=True)

        for h in range(MLSTM_HEADS):
            idx = d * MLSTM_HEADS + h
            li_c = li_c_all[:, h:h + 1]
            b_c = b_c_all[:, h:h + 1]
            li_r = li_r_all[h:h + 1, :]
            b_r = b_r_all[h:h + 1, :]
            g = g_all[:, h:h + 1]
            m_prev = m_sc[idx, 0:1, 0:1]
            n_prev = n_sc[idx, 0:1, :]
            ct = c_sc[idx]

            qh = q_ref[:, h * dk:(h + 1) * dk]
            kh = k_ref[:, h * dk:(h + 1) * dk]
            vh = v_ref[:, h * dv:(h + 1) * dv]
            k32 = kh.astype(F32)

            a_c = g - b_c + li_c
            m_new = jnp.maximum(g + m_prev, jnp.max(a_c, axis=0, keepdims=True))
            decay = jnp.exp(g + m_prev - m_new)
            w_c = jnp.exp(a_c - m_new)

            inter = b_c + m_prev
            logd = jnp.where(mask, b_c - b_r + li_r, NEG_BIG)
            m_c = jnp.maximum(inter, jnp.max(logd, axis=1, keepdims=True))
            p = jnp.exp(logd - m_c)
            qk = lax.dot_general(qh, kh, (((1,), (1,)), ((), ())), preferred_element_type=F32)
            s = qk * (p * scale)
            w_inter = jnp.exp(inter - m_c) * scale
            qc = jnp.dot(qh, ct.astype(BF16), preferred_element_type=F32)
            num = jnp.dot(s.astype(BF16), vh, preferred_element_type=F32) + w_inter * qc
            qn = jnp.sum(qh.astype(F32) * n_prev, axis=1, keepdims=True)
            den = jnp.sum(s, axis=1, keepdims=True) + w_inter * qn
            out_ref[:, h * dv:(h + 1) * dv] = num / jnp.maximum(jnp.abs(den), jnp.exp(-m_c))

            wv = (w_c * vh.astype(F32)).astype(BF16)
            kt = k32.T.astype(BF16)
            c_sc[idx] = decay * ct + jnp.dot(kt, wv, preferred_element_type=F32)
            n_new = decay * n_prev + jnp.sum(w_c * k32, axis=0, keepdims=True)
            n_sc[idx] = jnp.broadcast_to(n_new, (SUBLANES, dk))
            m_sc[idx] = jnp.broadcast_to(m_new, (SUBLANES, LANES))


def _mlstm(proj, gates, gates_t, batch, lp, pad, dk, dv):
    rows = proj.shape[0]
    n_chunks = lp // CHUNK
    d_qk = MLSTM_HEADS * dk
    d_v = MLSTM_HEADS * dv
    v_blk = (2 * d_qk) // d_v

    def fwd(b, c):
        return b * n_chunks + c

    def bwd(b, c):
        return b * n_chunks + n_chunks - 1 - c

    def specs(pos):
        return [
            pl.BlockSpec((CHUNK, d_qk), lambda b, c: (pos(b, c), 0)),
            pl.BlockSpec((CHUNK, d_qk), lambda b, c: (pos(b, c), 1)),
            pl.BlockSpec((CHUNK, d_v), lambda b, c: (pos(b, c), v_blk)),
            pl.BlockSpec((CHUNK, LANES), lambda b, c: (pos(b, c), 0)),
            pl.BlockSpec((FORGET_GATE_COLS, CHUNK), lambda b, c: (0, pos(b, c))),
        ]

    kern = functools.partial(_mlstm_kernel, pad, n_chunks, dk, dv)
    n_state = 2 * MLSTM_HEADS
    return pl.pallas_call(
        kern,
        grid=(batch, n_chunks),
        in_specs=specs(fwd) + specs(bwd),
        out_specs=[
            pl.BlockSpec((CHUNK, d_v), lambda b, c: (fwd(b, c), 0)),
            pl.BlockSpec((CHUNK, d_v), lambda b, c: (bwd(b, c), 0)),
        ],
        out_shape=[jax.ShapeDtypeStruct((rows, d_v), F32)] * 2,
        scratch_shapes=[
            pltpu.VMEM((n_state, dk, dv), F32),
            pltpu.VMEM((n_state, SUBLANES, dk), F32),
            pltpu.VMEM((n_state, SUBLANES, LANES), F32),
        ],
        compiler_params=_cparams("parallel", "arbitrary"),
        name="mlstm",
    )(proj, proj, proj, gates, gates_t, proj, proj, proj, gates, gates_t)


def _mlstm_out_kernel(dv, hf_ref, hb_ref, o_ref, ga_ref, nw_ref, w_ref, out_ref, y_sc):
    for h in range(MLSTM_HEADS):
        sl = slice(h * dv, (h + 1) * dv)
        hs = hf_ref[:, sl] + hb_ref[:, sl]
        r = lax.rsqrt(jnp.mean(hs * hs, axis=-1, keepdims=True) + EPS)
        y = hs * r * nw_ref[:, sl] * _sigmoid(o_ref[:, sl].astype(F32))
        y_sc[:, sl] = y.astype(BF16)
    ya = jnp.dot(y_sc[...], w_ref[...], preferred_element_type=F32)
    out_ref[...] = _sigmoid(ga_ref[...].astype(F32)) * ya


def _mlstm_out(hf, hb, proj, nw, w, dv, o_blk, ga_blk):
    rows, d_v = hf.shape
    d = w.shape[1]
    return pl.pallas_call(
        functools.partial(_mlstm_out_kernel, dv),
        grid=(rows // ROW_TILE,),
        in_specs=[
            pl.BlockSpec((ROW_TILE, d_v), lambda i: (i, 0)),
            pl.BlockSpec((ROW_TILE, d_v), lambda i: (i, 0)),
            pl.BlockSpec((ROW_TILE, d_v), lambda i: (i, o_blk)),
            pl.BlockSpec((ROW_TILE, d), lambda i: (i, ga_blk)),
            pl.BlockSpec((1, d_v), lambda i: (0, 0)),
            pl.BlockSpec((d_v, d), lambda i: (0, 0)),
        ],
        out_specs=pl.BlockSpec((ROW_TILE, d), lambda i: (i, 0)),
        out_shape=jax.ShapeDtypeStruct((rows, d), F32),
        scratch_shapes=[pltpu.VMEM((ROW_TILE, d_v), BF16)],
        compiler_params=_cparams("parallel"),
        name="mlstm_out",
    )(hf, hb, proj, proj, nw, w)


CONV_ROW_BLOCK = 128
CONV_COL_BLOCK = 256


def _conv_kernel(a_ref, g_ref, ap_ref, gp_ref, an_ref, gn_ref, cw_ref, cb_ref, lnw_ref, lnb_ref,
                 w_ref, bo_ref, gb_ref, ya_ref, out_ref, c_sc, conv_sc):
    i = pl.program_id(0)
    last = pl.num_programs(0) - 1

    def glu(a, g):
        return a.astype(F32) * _sigmoid(g.astype(F32))

    c_sc[HALO:HALO + ROW_TILE, :] = glu(a_ref[...], g_ref[...])
    c_sc[0:HALO, :] = jnp.where(i > 0, glu(ap_ref[...], gp_ref[...]), 0.0)
    c_sc[HALO + ROW_TILE:, :] = jnp.where(i < last, glu(an_ref[...], gn_ref[...]), 0.0)

    d = a_ref.shape[1]
    first = HALO - CONV_HALF
    span = CONV_ROW_BLOCK + 2 * HALO
    for rb in range(ROW_TILE // CONV_ROW_BLOCK):
        r0 = rb * CONV_ROW_BLOCK
        for cb in range(d // CONV_COL_BLOCK):
            cs = slice(cb * CONV_COL_BLOCK, (cb + 1) * CONV_COL_BLOCK)
            x = c_sc[r0:r0 + span, cs]
            acc = jnp.broadcast_to(cb_ref[:, cs], (CONV_ROW_BLOCK, CONV_COL_BLOCK))
            for sh in range(SUBLANES):
                xs = x if sh == 0 else pltpu.roll(x, span - sh, 0)
                for al in range((CONV_WIDTH + first) // SUBLANES + 1):
                    tap = al * SUBLANES + sh - first
                    if 0 <= tap < CONV_WIDTH:
                        acc = acc + cw_ref[tap:tap + 1, cs] * xs[al * SUBLANES:al * SUBLANES + CONV_ROW_BLOCK]
            conv_sc[r0:r0 + CONV_ROW_BLOCK, cs] = acc

    x = conv_sc[...]
    mu = jnp.mean(x, axis=-1, keepdims=True)
    xc = x - mu
    var = jnp.mean(xc * xc, axis=-1, keepdims=True)
    y = xc * lax.rsqrt(var + EPS) * lnw_ref[...] + lnb_ref[...]
    y = y * _sigmoid(y)
    yb = jnp.dot(y.astype(BF16), w_ref[...], preferred_element_type=F32) + bo_ref[...]
    out_ref[...] = ya_ref[...] + _sigmoid(gb_ref[...].astype(F32)) * yb


def _conv_merge(proj, ya, cw, cb, lnw, lnb, w, bo, a_blk, g_blk, gb_blk):
    rows = proj.shape[0]
    d = w.shape[0]
    n_tiles = rows // ROW_TILE
    per = ROW_TILE // HALO
    n_halo = rows // HALO
    a_col = a_blk
    g_col = g_blk

    def prev(i):
        return jnp.maximum(i * per - 1, 0)

    def nxt(i):
        return jnp.minimum((i + 1) * per, n_halo - 1)

    return pl.pallas_call(
        _conv_kernel,
        grid=(n_tiles,),
        in_specs=[
            pl.BlockSpec((ROW_TILE, d), lambda i: (i, a_col)),
            pl.BlockSpec((ROW_TILE, d), lambda i: (i, g_col)),
            pl.BlockSpec((HALO, d), lambda i: (prev(i), a_col)),
            pl.BlockSpec((HALO, d), lambda i: (prev(i), g_col)),
            pl.BlockSpec((HALO, d), lambda i: (nxt(i), a_col)),
            pl.BlockSpec((HALO, d), lambda i: (nxt(i), g_col)),
            pl.BlockSpec((CONV_WIDTH, d), lambda i: (0, 0)),
            pl.BlockSpec((1, d), lambda i: (0, 0)),
            pl.BlockSpec((1, d), lambda i: (0, 0)),
            pl.BlockSpec((1, d), lambda i: (0, 0)),
            pl.BlockSpec((d, d), lambda i: (0, 0)),
            pl.BlockSpec((1, d), lambda i: (0, 0)),
            pl.BlockSpec((ROW_TILE, d), lambda i: (i, gb_blk)),
            pl.BlockSpec((ROW_TILE, d), lambda i: (i, 0)),
        ],
        out_specs=pl.BlockSpec((ROW_TILE, d), lambda i: (i, 0)),
        out_shape=jax.ShapeDtypeStruct((rows, d), F32),
        scratch_shapes=[
            pltpu.VMEM((ROW_TILE + 2 * HALO, d), F32),
            pltpu.VMEM((ROW_TILE, d), F32),
        ],
        compiler_params=_cparams("parallel"),
        name="conv_merge",
    )(proj, proj, proj, proj, proj, proj, cw, cb, lnw, lnb, w, bo, proj, ya)


def _wo_kernel(m_ref, h_ref, w_ref, nw_ref, h2_ref, ut_ref):
    h2 = h_ref[...] + jnp.dot(m_ref[...].astype(BF16), w_ref[...], preferred_element_type=F32)
    h2_ref[...] = h2
    r = lax.rsqrt(jnp.mean(h2 * h2, axis=-1, keepdims=True) + EPS)
    u = h2 * r * nw_ref[...]
    ut_ref[...] = u.T.astype(BF16)


def _wo_residual(merged, hp, w, nw):
    rows, d = hp.shape
    return pl.pallas_call(
        _wo_kernel,
        grid=(rows // ROW_TILE,),
        in_specs=[
            pl.BlockSpec((ROW_TILE, d), lambda i: (i, 0)),
            pl.BlockSpec((ROW_TILE, d), lambda i: (i, 0)),
            pl.BlockSpec((d, d), lambda i: (0, 0)),
            pl.BlockSpec((1, d), lambda i: (0, 0)),
        ],
        out_specs=[
            pl.BlockSpec((ROW_TILE, d), lambda i: (i, 0)),
            pl.BlockSpec((d, ROW_TILE), lambda i: (0, i)),
        ],
        out_shape=[
            jax.ShapeDtypeStruct((rows, d), F32),
            jax.ShapeDtypeStruct((d, rows), BF16),
        ],
        compiler_params=_cparams("parallel"),
        name="wo_residual",
    )(merged, hp, w, nw)


def _top_values(x, k):
    vals = []
    for _ in range(k):
        mx = jnp.max(x, axis=0, keepdims=True)
        vals.append(mx)
        x = jnp.where(x == mx, -jnp.inf, x)
    return jnp.concatenate(vals, axis=0)


def _peer_route_kernel(n_keys, ut_ref, wq_ref, keys_ref, t1_ref, e1_ref, s2_ref, e2_ref):
    k = PEER_TOPK
    ut = ut_ref[...]
    tokens = ut.shape[1]
    for h in range(PEER_HEADS):
        tops = []
        scores = []
        for p in range(2):
            r0 = (2 * h + p) * n_keys
            q = jnp.dot(wq_ref[r0:r0 + n_keys, :], ut, preferred_element_type=F32)
            s = jnp.dot(keys_ref[p], q.astype(BF16), preferred_element_type=F32)
            scores.append(s)
            tops.append(_top_values(s, k))
        a1, a2 = tops
        n_y = [k // (x + 1) for x in range(k)]
        cand_rows = [a1[x:x + 1, :] + a2[0:n_y[x], :] for x in range(k)]
        fill = (-sum(n_y)) % SUBLANES
        if fill:
            cand_rows.append(jnp.full((fill, tokens), -jnp.inf, F32))
        top = _top_values(jnp.concatenate(cand_rows, axis=0), k)
        cmax = top[0:1, :]
        theta = top[k - 1:k, :]
        z = jnp.sum(jnp.exp(top - cmax), axis=0, keepdims=True)
        tau = jnp.full(scores[0].shape, jnp.inf, F32)
        for x in range(k):
            tau_x = jnp.min(jnp.where(cand_rows[x] >= theta, a2[0:n_y[x], :], jnp.inf), axis=0, keepdims=True)
            tau = jnp.where(scores[0] == a1[x:x + 1, :], tau_x, tau)
        t1_ref[h] = tau
        s2_ref[h] = scores[1]
        e1_ref[h] = jnp.exp(scores[0] - a1[0:1, :]) / z
        e2_ref[h] = jnp.exp(scores[1] - a2[0:1, :])


def _peer_route(ut, wq_t, keys):
    d, rows = ut.shape
    n_keys = keys.shape[1]
    tm = PEER_TOKEN_TILE
    big = pl.BlockSpec((PEER_HEADS, n_keys, tm), lambda t: (0, 0, t))
    big_shape = jax.ShapeDtypeStruct((PEER_HEADS, n_keys, rows), F32)
    return pl.pallas_call(
        functools.partial(_peer_route_kernel, n_keys),
        grid=(rows // tm,),
        in_specs=[
            pl.BlockSpec((d, tm), lambda t: (0, t)),
            pl.BlockSpec(wq_t.shape, lambda t: (0, 0)),
            pl.BlockSpec(keys.shape, lambda t: (0, 0, 0)),
        ],
        out_specs=[big, big, big, big],
        out_shape=[big_shape, big_shape, big_shape, big_shape],
        compiler_params=_cparams("parallel"),
        name="peer_route",
    )(ut, wq_t, keys)


PEER_GATE_ROWS = 64


def _peer_part(n_keys, part, tile, half, ut_ref, u_ref, vt_ref, gate_refs, out_ref, z_sc, act_new, act_old):
    t1_ref, e1_ref, s2_ref, e2_ref = gate_refs
    te, tm = PEER_EXPERT_TILE, PEER_TOKEN_TILE
    d = ut_ref.shape[0]
    gr = PEER_GATE_ROWS
    per = te // n_keys
    out_rows = d // per
    for il in range(per):
        i = tile * per + il
        c0 = il * n_keys
        z_sc[c0:c0 + n_keys, :] = jnp.dot(u_ref[part * te + c0:part * te + c0 + n_keys, :], ut_ref[...],
                                          preferred_element_type=F32)
        for jb in range(n_keys // gr):
            js = slice(jb * gr, (jb + 1) * gr)
            rows = slice(c0 + jb * gr, c0 + (jb + 1) * gr)
            for lc in range(tm // LANES):
                cols = slice(lc * LANES, (lc + 1) * LANES)
                gate = None
                for h in range(PEER_HEADS):
                    sel = jnp.where(s2_ref[h, js, cols] >= t1_ref[h, pl.ds(i, 1), :][:, cols],
                                    e2_ref[h, js, cols], 0.0)
                    term = sel * e1_ref[h, pl.ds(i, 1), :][:, cols]
                    gate = term if gate is None else gate + term
                act_new[rows, cols] = (gate * _gelu_tanh(z_sc[rows, cols], half)).astype(BF16)
        o0 = il * out_rows
        out_ref[o0:o0 + out_rows, :] += jnp.dot(vt_ref[o0:o0 + out_rows, :], act_old[...],
                                                preferred_element_type=F32)


def _peer_main_kernel(n_keys, n_pairs, ut_ref, u_ref, vta_ref, vtb_ref, t1_ref, e1_ref, s2_ref, e2_ref,
                      out_ref, z_sc, act0_sc, act1_sc):
    s = pl.program_id(1)
    last_tile = 2 * n_pairs - 1

    @pl.when(s == 0)
    def _():
        out_ref[...] = jnp.zeros_like(out_ref)
        act1_sc[...] = jnp.zeros_like(act1_sc)

    gate_refs = (t1_ref, e1_ref, s2_ref, e2_ref)
    half = jnp.where(s < n_pairs, 0.5, 0.0)
    _peer_part(n_keys, 0, jnp.minimum(2 * s, last_tile), half, ut_ref, u_ref, vta_ref, gate_refs, out_ref,
               z_sc, act0_sc, act1_sc)
    _peer_part(n_keys, 1, jnp.minimum(2 * s + 1, last_tile), half, ut_ref, u_ref, vtb_ref, gate_refs, out_ref,
               z_sc, act1_sc, act0_sc)


def _peer_main(ut, u_w, v_t, t1, e1, s2, e2):
    d, rows = ut.shape
    n_exp = u_w.shape[0]
    n_keys = t1.shape[1]
    tm, te = PEER_TOKEN_TILE, PEER_EXPERT_TILE
    n_pairs = n_exp // (2 * te)
    last_tile = 2 * n_pairs - 1
    big = pl.BlockSpec((PEER_HEADS, n_keys, tm), lambda t, s: (0, 0, t))
    return pl.pallas_call(
        functools.partial(_peer_main_kernel, n_keys, n_pairs),
        grid=(rows // tm, n_pairs + 1),
        in_specs=[
            pl.BlockSpec((d, tm), lambda t, s: (0, t)),
            pl.BlockSpec((2 * te, d), lambda t, s: (jnp.minimum(s, n_pairs - 1), 0)),
            pl.BlockSpec((d, te), lambda t, s: (0, jnp.maximum(2 * s - 1, 0))),
            pl.BlockSpec((d, te), lambda t, s: (0, jnp.minimum(2 * s, last_tile))),
            big, big, big, big,
        ],
        out_specs=pl.BlockSpec((d, tm), lambda t, s: (0, t)),
        out_shape=jax.ShapeDtypeStruct((d, rows), F32),
        scratch_shapes=[
            pltpu.VMEM((te, tm), F32),
            pltpu.VMEM((te, tm), BF16),
            pltpu.VMEM((te, tm), BF16),
        ],
        compiler_params=_cparams("parallel", "arbitrary"),
        name="peer_main",
    )(ut, u_w, v_t, v_t, t1, e1, s2, e2)


def _final_kernel(pt_ref, h2_ref, nw_ref, out_ref):
    h3 = h2_ref[...] + pt_ref[...].T
    r = lax.rsqrt(jnp.mean(h3 * h3, axis=-1, keepdims=True) + EPS)
    out_ref[0] = h3 * r * nw_ref[...]


def _final(peer_t, h2, nw, batch, seq, lp):
    d = h2.shape[1]
    tiles_per_batch = lp // OUT_TILE
    skip = tiles_per_batch - seq // OUT_TILE

    def pos(b, i):
        return b * tiles_per_batch + skip + i

    return pl.pallas_call(
        _final_kernel,
        grid=(batch, seq // OUT_TILE),
        in_specs=[
            pl.BlockSpec((d, OUT_TILE), lambda b, i: (0, pos(b, i))),
            pl.BlockSpec((OUT_TILE, d), lambda b, i: (pos(b, i), 0)),
            pl.BlockSpec((1, d), lambda b, i: (0, 0)),
        ],
        out_specs=pl.BlockSpec((1, OUT_TILE, d), lambda b, i: (b, i, 0)),
        out_shape=jax.ShapeDtypeStruct((batch, seq, d), F32),
        compiler_params=_cparams("parallel", "parallel"),
        name="final_norm",
    )(peer_t, h2, nw)


def _layer(hp, mask, batch, lp, pad, norm1_w, w_in, b_in, mlstm_norm_w, w_mlstm_out, conv_w, conv_b,
           conv_ln_w, conv_ln_b, w_conv_out, b_conv_out, w_o, norm2_w, peer_w_q, peer_sub_keys,
           peer_u, peer_v):
    d = hp.shape[1]
    d_v = w_mlstm_out.shape[0]
    dv = d_v // MLSTM_HEADS
    d_conv = w_conv_out.shape[0]
    d_qk = (w_in.shape[1] - 2 * d_v - FORGET_GATE_COLS - 2 * d_conv - 2 * d) // 2
    dk = d_qk // MLSTM_HEADS
    assert d_v == d and d_conv == d and 2 * d_qk == d_v, "column-block indexing assumes these widths"

    g0 = 2 * d_qk + 2 * d_v
    g1 = g0 + FORGET_GATE_COLS
    wm = jnp.concatenate([w_in[:, :g0], w_in[:, g1:]], axis=1).astype(BF16)
    bm = jnp.concatenate([b_in[:g0], b_in[g1:]])[None, :]
    wg = jnp.pad(w_in[:, g0:g1], ((0, 0), (0, LANES - FORGET_GATE_COLS))).astype(BF16)
    bg = jnp.pad(b_in[g0:g1], (0, LANES - FORGET_GATE_COLS))[None, :]
    o_blk, a_blk, g_blk, ga_blk, gb_blk = 2, 3, 4, 5, 6

    proj, gates = _inproj(hp, mask, norm1_w[None, :], wm, bm, wg, bg)
    gates_t = gates[:, :FORGET_GATE_COLS].T
    hf, hb = _mlstm(proj, gates, gates_t, batch, lp, pad, dk, dv)
    ya = _mlstm_out(hf, hb, proj, mlstm_norm_w[None, :], w_mlstm_out.astype(BF16), dv, o_blk, ga_blk)
    merged = _conv_merge(proj, ya, conv_w, conv_b[None, :], conv_ln_w[None, :], conv_ln_b[None, :],
                         w_conv_out.astype(BF16), b_conv_out[None, :], a_blk, g_blk, gb_blk)
    h2, ut = _wo_residual(merged, hp, w_o.astype(BF16), norm2_w[None, :])
    t1, e1, s2, e2 = _peer_route(ut, peer_w_q.T.astype(BF16), peer_sub_keys.astype(BF16))
    peer_t = _peer_main(ut, peer_u.astype(BF16), peer_v.T.astype(BF16), t1, e1, s2, e2)
    return h2, peer_t


def kernel(x, meta_tokens, norm1_w, w_in, b_in, mlstm_norm_w, w_mlstm_out, conv_w, conv_b, conv_ln_w,
           conv_ln_b, w_conv_out, b_conv_out, w_o, norm2_w, peer_w_q, peer_sub_keys, peer_u, peer_v,
           final_norm_w):
    batch, seq, d = x.shape
    depth = norm1_w.shape[0]
    assert depth == 1, "the fused final norm assumes a single layer"
    assert seq % OUT_TILE == 0
    seq_len = N_META + seq
    pad = HALO + (-(seq_len + HALO)) % CHUNK
    lp = pad + seq_len
    rows = batch * lp
    assert rows % ROW_TILE == 0 and rows % PEER_TOKEN_TILE == 0

    meta = jnp.broadcast_to(meta_tokens[None].astype(x.dtype), (batch, N_META, d))
    hp = jnp.concatenate([jnp.zeros((batch, pad, d), x.dtype), meta, x], axis=1).reshape(rows, d)
    mask = jnp.tile((jnp.arange(lp) >= pad).astype(F32), batch)[:, None]

    h2, peer_t = _layer(hp, mask, batch, lp, pad, norm1_w[0], w_in[0], b_in[0], mlstm_norm_w[0],
                        w_mlstm_out[0], conv_w[0], conv_b[0], conv_ln_w[0], conv_ln_b[0],
                        w_conv_out[0], b_conv_out[0], w_o[0], norm2_w[0], peer_w_q[0],
                        peer_sub_keys[0], peer_u[0], peer_v[0])
    return _final(peer_t, h2, final_norm_w[None, :], batch, seq, lp)
```

```python
import functools
import math

import jax
import jax.numpy as jnp
from jax import lax
from jax.experimental import pallas as pl
from jax.experimental.pallas import tpu as pltpu

F32 = jnp.float32
BF16 = jnp.bfloat16

LANES = 128
SUBLANES = 8
BF16_ROWS = 16
VMEM_LIMIT_BYTES = 56 * 1024 * 1024

N_META = 16
MLSTM_HEADS = 8
CONV_WIDTH = 31
CONV_HALF = CONV_WIDTH // 2
PEER_HEADS = 8
PEER_TOPK = 16
FORGET_GATE_COLS = 4 * MLSTM_HEADS
LOG_ZERO_GATE = -1.0e4
EPS = 1e-6
NEG_BIG = -1.0e30

ROW_TILE = 512
COL_TILE = 2048
CHUNK = 384
HALO = 16
PEER_TOKEN_TILE = 512
PEER_EXPERT_TILE = 512
OUT_TILE = 128


def _cparams(*sem):
    return pltpu.CompilerParams(dimension_semantics=sem, vmem_limit_bytes=VMEM_LIMIT_BYTES)


def _sigmoid(x):
    return 1.0 / (1.0 + jnp.exp(-x))


def _log_sigmoid(x):
    return jnp.minimum(x, 0.0) - jnp.log(1.0 + jnp.exp(-jnp.abs(x)))


def _gelu_tanh(x, half=0.5):
    c = math.sqrt(2.0 / math.pi)
    return half * x * (1.0 + jnp.tanh(c * (x + 0.044715 * (x * x * x))))


def _inproj_kernel(h_ref, mask_ref, nw_ref, wm_ref, bm_ref, wg_ref, bg_ref, proj_ref, gates_ref, u_sc):
    @pl.when(pl.program_id(1) == 0)
    def _():
        x = h_ref[...]
        r = lax.rsqrt(jnp.mean(x * x, axis=-1, keepdims=True) + EPS)
        u = (x * r * nw_ref[...]).astype(BF16)
        u_sc[...] = u
        g = jnp.dot(u, wg_ref[...], preferred_element_type=F32) + bg_ref[...]
        gates_ref[...] = g * mask_ref[...]

    acc = jnp.dot(u_sc[...], wm_ref[...], preferred_element_type=F32) + bm_ref[...]
    proj_ref[...] = (acc * mask_ref[...]).astype(BF16)


def _inproj(hp, mask, nw, wm, bm, wg, bg):
    rows, d = hp.shape
    n_main = wm.shape[1]
    grid = (rows // ROW_TILE, n_main // COL_TILE)
    return pl.pallas_call(
        _inproj_kernel,
        grid=grid,
        in_specs=[
            pl.BlockSpec((ROW_TILE, d), lambda i, j: (i, 0)),
            pl.BlockSpec((ROW_TILE, 1), lambda i, j: (i, 0)),
            pl.BlockSpec((1, d), lambda i, j: (0, 0)),
            pl.BlockSpec((d, COL_TILE), lambda i, j: (0, j)),
            pl.BlockSpec((1, COL_TILE), lambda i, j: (0, j)),
            pl.BlockSpec((d, LANES), lambda i, j: (0, 0)),
            pl.BlockSpec((1, LANES), lambda i, j: (0, 0)),
        ],
        out_specs=[
            pl.BlockSpec((ROW_TILE, COL_TILE), lambda i, j: (i, j)),
            pl.BlockSpec((ROW_TILE, LANES), lambda i, j: (i, 0)),
        ],
        out_shape=[
            jax.ShapeDtypeStruct((rows, n_main), BF16),
            jax.ShapeDtypeStruct((rows, LANES), F32),
        ],
        scratch_shapes=[pltpu.VMEM((ROW_TILE, d), BF16)],
        compiler_params=_cparams("parallel", "arbitrary"),
        name="inproj",
    )(hp, mask, nw, wm, bm, wg, bg)


def _mlstm_kernel(pad, n_chunks, dk, dv,
                  qf_ref, kf_ref, vf_ref, gf_ref, gtf_ref,
                  qb_ref, kb_ref, vb_ref, gb_ref, gtb_ref,
                  hf_ref, hb_ref, c_sc, n_sc, m_sc):
    c = pl.program_id(1)

    @pl.when(c == 0)
    def _():
        c_sc[...] = jnp.zeros_like(c_sc)
        n_sc[...] = jnp.zeros_like(n_sc)
        m_sc[...] = jnp.zeros_like(m_sc)

    scale = dk ** -0.5
    row = lax.broadcasted_iota(jnp.int32, (CHUNK, CHUNK), 0)
    col = lax.broadcasted_iota(jnp.int32, (CHUNK, CHUNK), 1)
    lower = col <= row
    upper = col >= row
    row1 = lax.broadcasted_iota(jnp.int32, (CHUNK, 1), 0)
    col1 = lax.broadcasted_iota(jnp.int32, (1, CHUNK), 1)

    dirs = (
        (qf_ref, kf_ref, vf_ref, gf_ref, gtf_ref, hf_ref, c, lower, upper),
        (qb_ref, kb_ref, vb_ref, gb_ref, gtb_ref, hb_ref, n_chunks - 1 - c, upper, lower),
    )
    for d, (q_ref, k_ref, v_ref, g_ref, gt_ref, out_ref, chunk, mask, mask_t) in enumerate(dirs):
        base = chunk * CHUNK
        valid_c = (base + row1) >= pad
        valid_r = (base + col1) >= pad
        g0 = 2 * MLSTM_HEADS * d
        gc = g_ref[...]
        gr = gt_ref[...]
        li_c_all = jnp.where(valid_c, gc[:, g0:g0 + MLSTM_HEADS], LOG_ZERO_GATE)
        lf_c_all = jnp.where(valid_c, _log_sigmoid(gc[:, g0 + MLSTM_HEADS:g0 + 2 * MLSTM_HEADS]), 0.0)
        li_r_all = jnp.where(valid_r, gr[g0:g0 + MLSTM_HEADS, :], LOG_ZERO_GATE)
        lf_r_all = jnp.where(valid_r, _log_sigmoid(gr[g0 + MLSTM_HEADS:g0 + 2 * MLSTM_HEADS, :]), 0.0)
        b_c_all = jnp.dot(mask.astype(F32), lf_c_all, precision=lax.Precision.HIGHEST,
                          preferred_element_type=F32)
        b_r_all = jnp.dot(lf_r_all, mask_t.astype(F32), precision=lax.Precision.HIGHEST,
                          preferred_element_type=F32)
        g_all = jnp.sum(lf_c_all, axis=0, keepdims=True)

        for h in range(MLSTM_HEADS):
            idx = d * MLSTM_HEADS + h
            li_c = li_c_all[:, h:h + 1]
            b_c = b_c_all[:, h:h + 1]
            li_r = li_r_all[h:h + 1, :]
            b_r = b_r_all[h:h + 1, :]
            g = g_all[:, h:h + 1]
            m_prev = m_sc[idx, 0:1, 0:1]
            n_prev = n_sc[idx, 0:1, :]
            ct = c_sc[idx]

            qh = q_ref[:, h * dk:(h + 1) * dk]
            kh = k_ref[:, h * dk:(h + 1) * dk]
            vh = v_ref[:, h * dv:(h + 1) * dv]
            k32 = kh.astype(F32)

            a_c = g - b_c + li_c
            m_new = jnp.maximum(g + m_prev, jnp.max(a_c, axis=0, keepdims=True))
            decay = jnp.exp(g + m_prev - m_new)
            w_c = jnp.exp(a_c - m_new)

            inter = b_c + m_prev
            logd = jnp.where(mask, b_c - b_r + li_r, NEG_BIG)
            m_c = jnp.maximum(inter, jnp.max(logd, axis=1, keepdims=True))
            p = jnp.exp(logd - m_c)
            qk = lax.dot_general(qh, kh, (((1,), (1,)), ((), ())), preferred_element_type=F32)
            s = qk * (p * scale)
            w_inter = jnp.exp(inter - m_c) * scale
            qc = jnp.dot(qh, ct.astype(BF16), preferred_element_type=F32)
            num = jnp.dot(s.astype(BF16), vh, preferred_element_type=F32) + w_inter * qc
            qn = jnp.sum(qh.astype(F32) * n_prev, axis=1, keepdims=True)
            den = jnp.sum(s, axis=1, keepdims=True) + w_inter * qn
            out_ref[:, h * dv:(h + 1) * dv] = num / jnp.maximum(jnp.abs(den), jnp.exp(-m_c))

            wv = (w_c * vh.astype(F32)).astype(BF16)
            kt = k32.T.astype(BF16)
            c_sc[idx] = decay * ct + jnp.dot(kt, wv, preferred_element_type=F32)
            n_new = decay * n_prev + jnp.sum(w_c * k32, axis=0, keepdims=True)
            n_sc[idx] = jnp.broadcast_to(n_new, (SUBLANES, dk))
            m_sc[idx] = jnp.broadcast_to(m_new, (SUBLANES, LANES))


def _mlstm(proj, gates, gates_t, batch, lp, pad, dk, dv):
    rows = proj.shape[0]
    n_chunks = lp // CHUNK
    d_qk = MLSTM_HEADS * dk
    d_v = MLSTM_HEADS * dv
    v_blk = (2 * d_qk) // d_v

    def fwd(b, c):
        return b * n_chunks + c

    def bwd(b, c):
        return b * n_chunks + n_chunks - 1 - c

    def specs(pos):
        return [
            pl.BlockSpec((CHUNK, d_qk), lambda b, c: (pos(b, c), 0)),
            pl.BlockSpec((CHUNK, d_qk), lambda b, c: (pos(b, c), 1)),
            pl.BlockSpec((CHUNK, d_v), lambda b, c: (pos(b, c), v_blk)),
            pl.BlockSpec((CHUNK, LANES), lambda b, c: (pos(b, c), 0)),
            pl.BlockSpec((FORGET_GATE_COLS, CHUNK), lambda b, c: (0, pos(b, c))),
        ]

    kern = functools.partial(_mlstm_kernel, pad, n_chunks, dk, dv)
    n_state = 2 * MLSTM_HEADS
    return pl.pallas_call(
        kern,
        grid=(batch, n_chunks),
        in_specs=specs(fwd) + specs(bwd),
        out_specs=[
            pl.BlockSpec((CHUNK, d_v), lambda b, c: (fwd(b, c), 0)),
            pl.BlockSpec((CHUNK, d_v), lambda b, c: (bwd(b, c), 0)),
        ],
        out_shape=[jax.ShapeDtypeStruct((rows, d_v), F32)] * 2,
        scratch_shapes=[
            pltpu.VMEM((n_state, dk, dv), F32),
            pltpu.VMEM((n_state, SUBLANES, dk), F32),
            pltpu.VMEM((n_state, SUBLANES, LANES), F32),
        ],
        compiler_params=_cparams("parallel", "arbitrary"),
        name="mlstm",
    )(proj, proj, proj, gates, gates_t, proj, proj, proj, gates, gates_t)


def _mlstm_out_kernel(dv, hf_ref, hb_ref, o_ref, ga_ref, nw_ref, w_ref, out_ref, y_sc):
    for h in range(MLSTM_HEADS):
        sl = slice(h * dv, (h + 1) * dv)
        hs = hf_ref[:, sl] + hb_ref[:, sl]
        r = lax.rsqrt(jnp.mean(hs * hs, axis=-1, keepdims=True) + EPS)
        y = hs * r * nw_ref[:, sl] * _sigmoid(o_ref[:, sl].astype(F32))
        y_sc[:, sl] = y.astype(BF16)
    ya = jnp.dot(y_sc[...], w_ref[...], preferred_element_type=F32)
    out_ref[...] = _sigmoid(ga_ref[...].astype(F32)) * ya


def _mlstm_out(hf, hb, proj, nw, w, dv, o_blk, ga_blk):
    rows, d_v = hf.shape
    d = w.shape[1]
    return pl.pallas_call(
        functools.partial(_mlstm_out_kernel, dv),
        grid=(rows // ROW_TILE,),
        in_specs=[
            pl.BlockSpec((ROW_TILE, d_v), lambda i: (i, 0)),
            pl.BlockSpec((ROW_TILE, d_v), lambda i: (i, 0)),
            pl.BlockSpec((ROW_TILE, d_v), lambda i: (i, o_blk)),
            pl.BlockSpec((ROW_TILE, d), lambda i: (i, ga_blk)),
            pl.BlockSpec((1, d_v), lambda i: (0, 0)),
            pl.BlockSpec((d_v, d), lambda i: (0, 0)),
        ],
        out_specs=pl.BlockSpec((ROW_TILE, d), lambda i: (i, 0)),
        out_shape=jax.ShapeDtypeStruct((rows, d), F32),
        scratch_shapes=[pltpu.VMEM((ROW_TILE, d_v), BF16)],
        compiler_params=_cparams("parallel"),
        name="mlstm_out",
    )(hf, hb, proj, proj, nw, w)


CONV_ROW_BLOCK = 128
CONV_COL_BLOCK = 256


def _conv_kernel(a_ref, g_ref, ap_ref, gp_ref, an_ref, gn_ref, cw_ref, cb_ref, lnw_ref, lnb_ref,
                 w_ref, bo_ref, gb_ref, ya_ref, out_ref, c_sc, conv_sc):
    i = pl.program_id(0)
    last = pl.num_programs(0) - 1

    def glu(a, g):
        return a.astype(F32) * _sigmoid(g.astype(F32))

    c_sc[HALO:HALO + ROW_TILE, :] = glu(a_ref[...], g_ref[...])
    c_sc[0:HALO, :] = jnp.where(i > 0, glu(ap_ref[...], gp_ref[...]), 0.0)
    c_sc[HALO + ROW_TILE:, :] = jnp.where(i < last, glu(an_ref[...], gn_ref[...]), 0.0)

    d = a_ref.shape[1]
    first = HALO - CONV_HALF
    span = CONV_ROW_BLOCK + 2 * HALO
    for rb in range(ROW_TILE // CONV_ROW_BLOCK):
        r0 = rb * CONV_ROW_BLOCK
        for cb in range(d // CONV_COL_BLOCK):
            cs = slice(cb * CONV_COL_BLOCK, (cb + 1) * CONV_COL_BLOCK)
            x = c_sc[r0:r0 + span, cs]
            acc = jnp.broadcast_to(cb_ref[:, cs], (CONV_ROW_BLOCK, CONV_COL_BLOCK))
            for sh in range(SUBLANES):
                xs = x if sh == 0 else pltpu.roll(x, span - sh, 0)
                for al in range((CONV_WIDTH + first) // SUBLANES + 1):
                    tap = al * SUBLANES + sh - first
                    if 0 <= tap < CONV_WIDTH:
                        acc = acc + cw_ref[tap:tap + 1, cs] * xs[al * SUBLANES:al * SUBLANES + CONV_ROW_BLOCK]
            conv_sc[r0:r0 + CONV_ROW_BLOCK, cs] = acc

    x = conv_sc[...]
    mu = jnp.mean(x, axis=-1, keepdims=True)
    xc = x - mu
    var = jnp.mean(xc * xc, axis=-1, keepdims=True)
    y = xc * lax.rsqrt(var + EPS) * lnw_ref[...] + lnb_ref[...]
    y = y * _sigmoid(y)
    yb = jnp.dot(y.astype(BF16), w_ref[...], preferred_element_type=F32) + bo_ref[...]
    out_ref[...] = ya_ref[...] + _sigmoid(gb_ref[...].astype(F32)) * yb


def _conv_merge(proj, ya, cw, cb, lnw, lnb, w, bo, a_blk, g_blk, gb_blk):
    rows = proj.shape[0]
    d = w.shape[0]
    n_tiles = rows // ROW_TILE
    per = ROW_TILE // HALO
    n_halo = rows // HALO
    a_col = a_blk
    g_col = g_blk

    def prev(i):
        return jnp.maximum(i * per - 1, 0)

    def nxt(i):
        return jnp.minimum((i + 1) * per, n_halo - 1)

    return pl.pallas_call(
        _conv_kernel,
        grid=(n_tiles,),
        in_specs=[
            pl.BlockSpec((ROW_TILE, d), lambda i: (i, a_col)),
            pl.BlockSpec((ROW_TILE, d), lambda i: (i, g_col)),
            pl.BlockSpec((HALO, d), lambda i: (prev(i), a_col)),
            pl.BlockSpec((HALO, d), lambda i: (prev(i), g_col)),
            pl.BlockSpec((HALO, d), lambda i: (nxt(i), a_col)),
            pl.BlockSpec((HALO, d), lambda i: (nxt(i), g_col)),
            pl.BlockSpec((CONV_WIDTH, d), lambda i: (0, 0)),
            pl.BlockSpec((1, d), lambda i: (0, 0)),
            pl.BlockSpec((1, d), lambda i: (0, 0)),
            pl.BlockSpec((1, d), lambda i: (0, 0)),
            pl.BlockSpec((d, d), lambda i: (0, 0)),
            pl.BlockSpec((1, d), lambda i: (0, 0)),
            pl.BlockSpec((ROW_TILE, d), lambda i: (i, gb_blk)),
            pl.BlockSpec((ROW_TILE, d), lambda i: (i, 0)),
        ],
        out_specs=pl.BlockSpec((ROW_TILE, d), lambda i: (i, 0)),
        out_shape=jax.ShapeDtypeStruct((rows, d), F32),
        scratch_shapes=[
            pltpu.VMEM((ROW_TILE + 2 * HALO, d), F32),
            pltpu.VMEM((ROW_TILE, d), F32),
        ],
        compiler_params=_cparams("parallel"),
        name="conv_merge",
    )(proj, proj, proj, proj, proj, proj, cw, cb, lnw, lnb, w, bo, proj, ya)


def _wo_kernel(m_ref, h_ref, w_ref, nw_ref, h2_ref, ut_ref):
    h2 = h_ref[...] + jnp.dot(m_ref[...].astype(BF16), w_ref[...], preferred_element_type=F32)
    h2_ref[...] = h2
    r = lax.rsqrt(jnp.mean(h2 * h2, axis=-1, keepdims=True) + EPS)
    u = h2 * r * nw_ref[...]
    ut_ref[...] = u.T.astype(BF16)


def _wo_residual(merged, hp, w, nw):
    rows, d = hp.shape
    return pl.pallas_call(
        _wo_kernel,
        grid=(rows // ROW_TILE,),
        in_specs=[
            pl.BlockSpec((ROW_TILE, d), lambda i: (i, 0)),
            pl.BlockSpec((ROW_TILE, d), lambda i: (i, 0)),
            pl.BlockSpec((d, d), lambda i: (0, 0)),
            pl.BlockSpec((1, d), lambda i: (0, 0)),
        ],
        out_specs=[
            pl.BlockSpec((ROW_TILE, d), lambda i: (i, 0)),
            pl.BlockSpec((d, ROW_TILE), lambda i: (0, i)),
        ],
        out_shape=[
            jax.ShapeDtypeStruct((rows, d), F32),
            jax.ShapeDtypeStruct((d, rows), BF16),
        ],
        compiler_params=_cparams("parallel"),
        name="wo_residual",
    )(merged, hp, w, nw)


def _top_values(x, k):
    vals = []
    for _ in range(k):
        mx = jnp.max(x, axis=0, keepdims=True)
        vals.append(mx)
        x = jnp.where(x == mx, -jnp.inf, x)
    return jnp.concatenate(vals, axis=0)


def _odd_even_merge_sort_pairs(n):
    pairs = []

    def merge(lo, size, r):
        step = 2 * r
        if step < size:
            merge(lo, size, step)
            merge(lo + r, size, step)
            pairs.extend((i, i + r) for i in range(lo + r, lo + size - r, step))
        else:
            pairs.append((lo, lo + r))

    def sort(lo, size):
        if size > 1:
            sort(lo, size // 2)
            sort(lo + size // 2, size // 2)
            merge(lo, size, 1)

    sort(0, n)
    return pairs


def _compare_exchange(v, i, j):
    v[i], v[j] = jnp.maximum(v[i], v[j]), jnp.minimum(v[i], v[j])


def _top_values_network(x, k):
    assert x.shape[0] == SUBLANES * k and k & (k - 1) == 0
    v = [x[r * SUBLANES:(r + 1) * SUBLANES, :] for r in range(k)]
    for i, j in _odd_even_merge_sort_pairs(k):
        _compare_exchange(v, i, j)
    shift = SUBLANES // 2
    while shift:
        v = [jnp.maximum(v[i], pltpu.roll(v[k - 1 - i], shift, 0)) for i in range(k)]
        stride = k // 2
        while stride:
            for i in range(k):
                if i & stride == 0:
                    _compare_exchange(v, i, i + stride)
            stride //= 2
        shift //= 2
    return jnp.concatenate([b[0:1, :] for b in v], axis=0)


def _peer_route_kernel(n_keys, ut_ref, wq_ref, keys_ref, t1_ref, e1_ref, s2_ref, e2_ref):
    k = PEER_TOPK
    ut = ut_ref[...]
    tokens = ut.shape[1]
    for h in range(PEER_HEADS):
        tops = []
        scores = []
        for p in range(2):
            r0 = (2 * h + p) * n_keys
            q = jnp.dot(wq_ref[r0:r0 + n_keys, :], ut, preferred_element_type=F32)
            s = jnp.dot(keys_ref[p], q.astype(BF16), preferred_element_type=F32)
            scores.append(s)
            tops.append(_top_values_network(s, k))
        a1, a2 = tops
        n_y = [k // (x + 1) for x in range(k)]
        cand_rows = [a1[x:x + 1, :] + a2[0:n_y[x], :] for x in range(k)]
        fill = (-sum(n_y)) % SUBLANES
        if fill:
            cand_rows.append(jnp.full((fill, tokens), -jnp.inf, F32))
        top = _top_values(jnp.concatenate(cand_rows, axis=0), k)
        cmax = top[0:1, :]
        theta = top[k - 1:k, :]
        z = jnp.sum(jnp.exp(top - cmax), axis=0, keepdims=True)
        tau = jnp.full(scores[0].shape, jnp.inf, F32)
        for x in range(k):
            tau_x = jnp.min(jnp.where(cand_rows[x] >= theta, a2[0:n_y[x], :], jnp.inf), axis=0, keepdims=True)
            tau = jnp.where(scores[0] == a1[x:x + 1, :], tau_x, tau)
        t1_ref[h] = tau
        s2_ref[h] = scores[1]
        e1_ref[h] = jnp.exp(scores[0] - a1[0:1, :]) / z
        e2_ref[h] = jnp.exp(scores[1] - a2[0:1, :])


def _peer_route(ut, wq_t, keys):
    d, rows = ut.shape
    n_keys = keys.shape[1]
    tm = PEER_TOKEN_TILE
    big = pl.BlockSpec((PEER_HEADS, n_keys, tm), lambda t: (0, 0, t))
    big_shape = jax.ShapeDtypeStruct((PEER_HEADS, n_keys, rows), F32)
    return pl.pallas_call(
        functools.partial(_peer_route_kernel, n_keys),
        grid=(rows // tm,),
        in_specs=[
            pl.BlockSpec((d, tm), lambda t: (0, t)),
            pl.BlockSpec(wq_t.shape, lambda t: (0, 0)),
            pl.BlockSpec(keys.shape, lambda t: (0, 0, 0)),
        ],
        out_specs=[big, big, big, big],
        out_shape=[big_shape, big_shape, big_shape, big_shape],
        compiler_params=_cparams("parallel"),
        name="peer_route",
    )(ut, wq_t, keys)


PEER_GATE_ROWS = 64


def _peer_part(n_keys, part, tile, half, ut_ref, u_ref, vt_ref, gate_refs, out_ref, z_sc, act_new, act_old):
    t1_ref, e1_ref, s2_ref, e2_ref = gate_refs
    te, tm = PEER_EXPERT_TILE, PEER_TOKEN_TILE
    d = ut_ref.shape[0]
    gr = PEER_GATE_ROWS
    per = te // n_keys
    out_rows = d // per
    for il in range(per):
        i = tile * per + il
        c0 = il * n_keys
        z_sc[c0:c0 + n_keys, :] = jnp.dot(u_ref[part * te + c0:part * te + c0 + n_keys, :], ut_ref[...],
                                          preferred_element_type=F32)
        for jb in range(n_keys // gr):
            js = slice(jb * gr, (jb + 1) * gr)
            rows = slice(c0 + jb * gr, c0 + (jb + 1) * gr)
            for lc in range(tm // LANES):
                cols = slice(lc * LANES, (lc + 1) * LANES)
                gate = None
                for h in range(PEER_HEADS):
                    sel = jnp.where(s2_ref[h, js, cols] >= t1_ref[h, pl.ds(i, 1), :][:, cols],
                                    e2_ref[h, js, cols], 0.0)
                    term = sel * e1_ref[h, pl.ds(i, 1), :][:, cols]
                    gate = term if gate is None else gate + term
                act_new[rows, cols] = (gate * _gelu_tanh(z_sc[rows, cols], half)).astype(BF16)
        o0 = il * out_rows
        out_ref[o0:o0 + out_rows, :] += jnp.dot(vt_ref[o0:o0 + out_rows, :], act_old[...],
                                                preferred_element_type=F32)


def _peer_main_kernel(n_keys, n_pairs, ut_ref, u_ref, vta_ref, vtb_ref, t1_ref, e1_ref, s2_ref, e2_ref,
                      out_ref, z_sc, act0_sc, act1_sc):
    s = pl.program_id(1)
    last_tile = 2 * n_pairs - 1

    @pl.when(s == 0)
    def _():
        out_ref[...] = jnp.zeros_like(out_ref)
        act1_sc[...] = jnp.zeros_like(act1_sc)

    gate_refs = (t1_ref, e1_ref, s2_ref, e2_ref)
    half = jnp.where(s < n_pairs, 0.5, 0.0)
    _peer_part(n_keys, 0, jnp.minimum(2 * s, last_tile), half, ut_ref, u_ref, vta_ref, gate_refs, out_ref,
               z_sc, act0_sc, act1_sc)
    _peer_part(n_keys, 1, jnp.minimum(2 * s + 1, last_tile), half, ut_ref, u_ref, vtb_ref, gate_refs, out_ref,
               z_sc, act1_sc, act0_sc)


def _peer_main(ut, u_w, v_t, t1, e1, s2, e2):
    d, rows = ut.shape
    n_exp = u_w.shape[0]
    n_keys = t1.shape[1]
    tm, te = PEER_TOKEN_TILE, PEER_EXPERT_TILE
    n_pairs = n_exp // (2 * te)
    last_tile = 2 * n_pairs - 1
    big = pl.BlockSpec((PEER_HEADS, n_keys, tm), lambda t, s: (0, 0, t))
    return pl.pallas_call(
        functools.partial(_peer_main_kernel, n_keys, n_pairs),
        grid=(rows // tm, n_pairs + 1),
        in_specs=[
            pl.BlockSpec((d, tm), lambda t, s: (0, t)),
            pl.BlockSpec((2 * te, d), lambda t, s: (jnp.minimum(s, n_pairs - 1), 0)),
            pl.BlockSpec((d, te), lambda t, s: (0, jnp.maximum(2 * s - 1, 0))),
            pl.BlockSpec((d, te), lambda t, s: (0, jnp.minimum(2 * s, last_tile))),
            big, big, big, big,
        ],
        out_specs=pl.BlockSpec((d, tm), lambda t, s: (0, t)),
        out_shape=jax.ShapeDtypeStruct((d, rows), F32),
        scratch_shapes=[
            pltpu.VMEM((te, tm), F32),
            pltpu.VMEM((te, tm), BF16),
            pltpu.VMEM((te, tm), BF16),
        ],
        compiler_params=_cparams("parallel", "arbitrary"),
        name="peer_main",
    )(ut, u_w, v_t, v_t, t1, e1, s2, e2)


def _final_kernel(pt_ref, h2_ref, nw_ref, out_ref):
    h3 = h2_ref[...] + pt_ref[...].T
    r = lax.rsqrt(jnp.mean(h3 * h3, axis=-1, keepdims=True) + EPS)
    out_ref[0] = h3 * r * nw_ref[...]


def _final(peer_t, h2, nw, batch, seq, lp):
    d = h2.shape[1]
    tiles_per_batch = lp // OUT_TILE
    skip = tiles_per_batch - seq // OUT_TILE

    def pos(b, i):
        return b * tiles_per_batch + skip + i

    return pl.pallas_call(
        _final_kernel,
        grid=(batch, seq // OUT_TILE),
        in_specs=[
            pl.BlockSpec((d, OUT_TILE), lambda b, i: (0, pos(b, i))),
            pl.BlockSpec((OUT_TILE, d), lambda b, i: (pos(b, i), 0)),
            pl.BlockSpec((1, d), lambda b, i: (0, 0)),
        ],
        out_specs=pl.BlockSpec((1, OUT_TILE, d), lambda b, i: (b, i, 0)),
        out_shape=jax.ShapeDtypeStruct((batch, seq, d), F32),
        compiler_params=_cparams("parallel", "parallel"),
        name="final_norm",
    )(peer_t, h2, nw)


def _layer(hp, mask, batch, lp, pad, norm1_w, w_in, b_in, mlstm_norm_w, w_mlstm_out, conv_w, conv_b,
           conv_ln_w, conv_ln_b, w_conv_out, b_conv_out, w_o, norm2_w, peer_w_q, peer_sub_keys,
           peer_u, peer_v):
    d = hp.shape[1]
    d_v = w_mlstm_out.shape[0]
    dv = d_v // MLSTM_HEADS
    d_conv = w_conv_out.shape[0]
    d_qk = (w_in.shape[1] - 2 * d_v - FORGET_GATE_COLS - 2 * d_conv - 2 * d) // 2
    dk = d_qk // MLSTM_HEADS
    assert d_v == d and d_conv == d and 2 * d_qk == d_v, "column-block indexing assumes these widths"

    g0 = 2 * d_qk + 2 * d_v
    g1 = g0 + FORGET_GATE_COLS
    wm = jnp.concatenate([w_in[:, :g0], w_in[:, g1:]], axis=1).astype(BF16)
    bm = jnp.concatenate([b_in[:g0], b_in[g1:]])[None, :]
    wg = jnp.pad(w_in[:, g0:g1], ((0, 0), (0, LANES - FORGET_GATE_COLS))).astype(BF16)
    bg = jnp.pad(b_in[g0:g1], (0, LANES - FORGET_GATE_COLS))[None, :]
    o_blk, a_blk, g_blk, ga_blk, gb_blk = 2, 3, 4, 5, 6

    proj, gates = _inproj(hp, mask, norm1_w[None, :], wm, bm, wg, bg)
    gates_t = gates[:, :FORGET_GATE_COLS].T
    hf, hb = _mlstm(proj, gates, gates_t, batch, lp, pad, dk, dv)
    ya = _mlstm_out(hf, hb, proj, mlstm_norm_w[None, :], w_mlstm_out.astype(BF16), dv, o_blk, ga_blk)
    merged = _conv_merge(proj, ya, conv_w, conv_b[None, :], conv_ln_w[None, :], conv_ln_b[None, :],
                         w_conv_out.astype(BF16), b_conv_out[None, :], a_blk, g_blk, gb_blk)
    h2, ut = _wo_residual(merged, hp, w_o.astype(BF16), norm2_w[None, :])
    t1, e1, s2, e2 = _peer_route(ut, peer_w_q.T.astype(BF16), peer_sub_keys.astype(BF16))
    peer_t = _peer_main(ut, peer_u.astype(BF16), peer_v.T.astype(BF16), t1, e1, s2, e2)
    return h2, peer_t


def kernel(x, meta_tokens, norm1_w, w_in, b_in, mlstm_norm_w, w_mlstm_out, conv_w, conv_b, conv_ln_w,
           conv_ln_b, w_conv_out, b_conv_out, w_o, norm2_w, peer_w_q, peer_sub_keys, peer_u, peer_v,
           final_norm_w):
    batch, seq, d = x.shape
    depth = norm1_w.shape[0]
    assert depth == 1, "the fused final norm assumes a single layer"
    assert seq % OUT_TILE == 0
    seq_len = N_META + seq
    pad = HALO + (-(seq_len + HALO)) % CHUNK
    lp = pad + seq_len
    rows = batch * lp
    assert rows % ROW_TILE == 0 and rows % PEER_TOKEN_TILE == 0

    meta = jnp.broadcast_to(meta_tokens[None].astype(x.dtype), (batch, N_META, d))
    hp = jnp.concatenate([jnp.zeros((batch, pad, d), x.dtype), meta, x], axis=1).reshape(rows, d)
    mask = jnp.tile((jnp.arange(lp) >= pad).astype(F32), batch)[:, None]

    h2, peer_t = _layer(hp, mask, batch, lp, pad, norm1_w[0], w_in[0], b_in[0], mlstm_norm_w[0],
                        w_mlstm_out[0], conv_w[0], conv_b[0], conv_ln_w[0], conv_ln_b[0],
                        w_conv_out[0], b_conv_out[0], w_o[0], norm2_w[0], peer_w_q[0],
                        peer_sub_keys[0], peer_u[0], peer_v[0])
    return _final(peer_t, h2, final_norm_w[None, :], batch, seq, lp)
```

```python
import functools
import math

import jax
import jax.numpy as jnp
from jax import lax
from jax.experimental import pallas as pl
from jax.experimental.pallas import tpu as pltpu

F32 = jnp.float32
BF16 = jnp.bfloat16

LANES = 128
SUBLANES = 8
BF16_ROWS = 16
VMEM_LIMIT_BYTES = 56 * 1024 * 1024

N_META = 16
MLSTM_HEADS = 8
CONV_WIDTH = 31
CONV_HALF = CONV_WIDTH // 2
PEER_HEADS = 8
PEER_TOPK = 16
FORGET_GATE_COLS = 4 * MLSTM_HEADS
LOG_ZERO_GATE = -1.0e4
EPS = 1e-6
NEG_BIG = -1.0e30

ROW_TILE = 512
COL_TILE = 2048
CHUNK = 384
HALO = 16
PEER_TOKEN_TILE = 512
PEER_EXPERT_TILE = 512
OUT_TILE = 128


def _cparams(*sem):
    return pltpu.CompilerParams(dimension_semantics=sem, vmem_limit_bytes=VMEM_LIMIT_BYTES)


def _sigmoid(x):
    return 1.0 / (1.0 + jnp.exp(-x))


def _log_sigmoid(x):
    return jnp.minimum(x, 0.0) - jnp.log(1.0 + jnp.exp(-jnp.abs(x)))


def _gelu_tanh(x):
    c = math.sqrt(2.0 / math.pi)
    return 0.5 * x * (1.0 + jnp.tanh(c * (x + 0.044715 * (x * x * x))))


def _inproj_kernel(h_ref, mask_ref, nw_ref, wm_ref, bm_ref, wg_ref, bg_ref, proj_ref, gates_ref, u_sc):
    @pl.when(pl.program_id(1) == 0)
    def _():
        x = h_ref[...]
        r = lax.rsqrt(jnp.mean(x * x, axis=-1, keepdims=True) + EPS)
        u = (x * r * nw_ref[...]).astype(BF16)
        u_sc[...] = u
        g = jnp.dot(u, wg_ref[...], preferred_element_type=F32) + bg_ref[...]
        gates_ref[...] = g * mask_ref[...]

    acc = jnp.dot(u_sc[...], wm_ref[...], preferred_element_type=F32) + bm_ref[...]
    proj_ref[...] = (acc * mask_ref[...]).astype(BF16)


def _inproj(hp, mask, nw, wm, bm, wg, bg):
    rows, d = hp.shape
    n_main = wm.shape[1]
    grid = (rows // ROW_TILE, n_main // COL_TILE)
    return pl.pallas_call(
        _inproj_kernel,
        grid=grid,
        in_specs=[
            pl.BlockSpec((ROW_TILE, d), lambda i, j: (i, 0)),
            pl.BlockSpec((ROW_TILE, 1), lambda i, j: (i, 0)),
            pl.BlockSpec((1, d), lambda i, j: (0, 0)),
            pl.BlockSpec((d, COL_TILE), lambda i, j: (0, j)),
            pl.BlockSpec((1, COL_TILE), lambda i, j: (0, j)),
            pl.BlockSpec((d, LANES), lambda i, j: (0, 0)),
            pl.BlockSpec((1, LANES), lambda i, j: (0, 0)),
        ],
        out_specs=[
            pl.BlockSpec((ROW_TILE, COL_TILE), lambda i, j: (i, j)),
            pl.BlockSpec((ROW_TILE, LANES), lambda i, j: (i, 0)),
        ],
        out_shape=[
            jax.ShapeDtypeStruct((rows, n_main), BF16),
            jax.ShapeDtypeStruct((rows, LANES), F32),
        ],
        scratch_shapes=[pltpu.VMEM((ROW_TILE, d), BF16)],
        compiler_params=_cparams("parallel", "arbitrary"),
        name="inproj",
    )(hp, mask, nw, wm, bm, wg, bg)


def _mlstm_kernel(pad, n_chunks, dk, dv,
                  qf_ref, kf_ref, vf_ref, gf_ref, gtf_ref,
                  qb_ref, kb_ref, vb_ref, gb_ref, gtb_ref,
                  hf_ref, hb_ref, c_sc, n_sc, m_sc):
    c = pl.program_id(1)

    @pl.when(c == 0)
    def _():
        c_sc[...] = jnp.zeros_like(c_sc)
        n_sc[...] = jnp.zeros_like(n_sc)
        m_sc[...] = jnp.zeros_like(m_sc)

    scale = dk ** -0.5
    row = lax.broadcasted_iota(jnp.int32, (CHUNK, CHUNK), 0)
    col = lax.broadcasted_iota(jnp.int32, (CHUNK, CHUNK), 1)
    lower = col <= row
    upper = col >= row
    row1 = lax.broadcasted_iota(jnp.int32, (CHUNK, 1), 0)
    col1 = lax.broadcasted_iota(jnp.int32, (1, CHUNK), 1)

    dirs = (
        (qf_ref, kf_ref, vf_ref, gf_ref, gtf_ref, hf_ref, c, lower, upper),
        (qb_ref, kb_ref, vb_ref, gb_ref, gtb_ref, hb_ref, n_chunks - 1 - c, upper, lower),
    )
    for d, (q_ref, k_ref, v_ref, g_ref, gt_ref, out_ref, chunk, mask, mask_t) in enumerate(dirs):
        base = chunk * CHUNK
        valid_c = (base + row1) >= pad
        valid_r = (base + col1) >= pad
        g0 = 2 * MLSTM_HEADS * d
        gc = g_ref[...]
        gr = gt_ref[...]
        li_c_all = jnp.where(valid_c, gc[:, g0:g0 + MLSTM_HEADS], LOG_ZERO_GATE)
        lf_c_all = jnp.where(valid_c, _log_sigmoid(gc[:, g0 + MLSTM_HEADS:g0 + 2 * MLSTM_HEADS]), 0.0)
        li_r_all = jnp.where(valid_r, gr[g0:g0 + MLSTM_HEADS, :], LOG_ZERO_GATE)
        lf_r_all = jnp.where(valid_r, _log_sigmoid(gr[g0 + MLSTM_HEADS:g0 + 2 * MLSTM_HEADS, :]), 0.0)
        b_c_all = jnp.dot(mask.astype(F32), lf_c_all, precision=lax.Precision.HIGHEST,
                          preferred_element_type=F32)
        b_r_all = jnp.dot(lf_r_all, mask_t.astype(F32), precision=lax.Precision.HIGHEST,
                          preferred_element_type=F32)
        g_all = jnp.sum(lf_c_all, axis=0, keepdims=True)

        for h in range(MLSTM_HEADS):
            idx = d * MLSTM_HEADS + h
            li_c = li_c_all[:, h:h + 1]
            b_c = b_c_all[:, h:h + 1]
            li_r = li_r_all[h:h + 1, :]
            b_r = b_r_all[h:h + 1, :]
            g = g_all[:, h:h + 1]
            m_prev = m_sc[idx, 0:1, 0:1]
            n_prev = n_sc[idx, 0:1, :]
            ct = c_sc[idx]

            qh = q_ref[:, h * dk:(h + 1) * dk]
            kh = k_ref[:, h * dk:(h + 1) * dk]
            vh = v_ref[:, h * dv:(h + 1) * dv]
            k32 = kh.astype(F32)

            a_c = g - b_c + li_c
            m_new = jnp.maximum(g + m_prev, jnp.max(a_c, axis=0, keepdims=True))
            decay = jnp.exp(g + m_prev - m_new)
            w_c = jnp.exp(a_c - m_new)

            inter = b_c + m_prev
            logd = jnp.where(mask, b_c - b_r + li_r, NEG_BIG)
            m_c = jnp.maximum(inter, jnp.max(logd, axis=1, keepdims=True))
            p = jnp.exp(logd - m_c)
            qk = lax.dot_general(qh, kh, (((1,), (1,)), ((), ())), preferred_element_type=F32)
            s = qk * (p * scale)
            w_inter = jnp.exp(inter - m_c) * scale
            qc = jnp.dot(qh, ct.astype(BF16), preferred_element_type=F32)
            num = jnp.dot(s.astype(BF16), vh, preferred_element_type=F32) + w_inter * qc
            qn = jnp.sum(qh.astype(F32) * n_prev, axis=1, keepdims=True)
            den = jnp.sum(s, axis=1, keepdims=True) + w_inter * qn
            out_ref[:, h * dv:(h + 1) * dv] = num / jnp.maximum(jnp.abs(den), jnp.exp(-m_c))

            wv = (w_c * vh.astype(F32)).astype(BF16)
            kt = k32.T.astype(BF16)
            c_sc[idx] = decay * ct + jnp.dot(kt, wv, preferred_element_type=F32)
            n_new = decay * n_prev + jnp.sum(w_c * k32, axis=0, keepdims=True)
            n_sc[idx] = jnp.broadcast_to(n_new, (SUBLANES, dk))
            m_sc[idx] = jnp.broadcast_to(m_new, (SUBLANES, LANES))


def _mlstm(proj, gates, gates_t, batch, lp, pad, dk, dv):
    rows = proj.shape[0]
    n_chunks = lp // CHUNK
    d_qk = MLSTM_HEADS * dk
    d_v = MLSTM_HEADS * dv
    v_blk = (2 * d_qk) // d_v

    def fwd(b, c):
        return b * n_chunks + c

    def bwd(b, c):
        return b * n_chunks + n_chunks - 1 - c

    def specs(pos):
        return [
            pl.BlockSpec((CHUNK, d_qk), lambda b, c: (pos(b, c), 0)),
            pl.BlockSpec((CHUNK, d_qk), lambda b, c: (pos(b, c), 1)),
            pl.BlockSpec((CHUNK, d_v), lambda b, c: (pos(b, c), v_blk)),
            pl.BlockSpec((CHUNK, LANES), lambda b, c: (pos(b, c), 0)),
            pl.BlockSpec((FORGET_GATE_COLS, CHUNK), lambda b, c: (0, pos(b, c))),
        ]

    kern = functools.partial(_mlstm_kernel, pad, n_chunks, dk, dv)
    n_state = 2 * MLSTM_HEADS
    return pl.pallas_call(
        kern,
        grid=(batch, n_chunks),
        in_specs=specs(fwd) + specs(bwd),
        out_specs=[
            pl.BlockSpec((CHUNK, d_v), lambda b, c: (fwd(b, c), 0)),
            pl.BlockSpec((CHUNK, d_v), lambda b, c: (bwd(b, c), 0)),
        ],
        out_shape=[jax.ShapeDtypeStruct((rows, d_v), F32)] * 2,
        scratch_shapes=[
            pltpu.VMEM((n_state, dk, dv), F32),
            pltpu.VMEM((n_state, SUBLANES, dk), F32),
            pltpu.VMEM((n_state, SUBLANES, LANES), F32),
        ],
        compiler_params=_cparams("parallel", "arbitrary"),
        name="mlstm",
    )(proj, proj, proj, gates, gates_t, proj, proj, proj, gates, gates_t)


def _mlstm_out_kernel(dv, hf_ref, hb_ref, o_ref, ga_ref, nw_ref, w_ref, out_ref, y_sc):
    for h in range(MLSTM_HEADS):
        sl = slice(h * dv, (h + 1) * dv)
        hs = hf_ref[:, sl] + hb_ref[:, sl]
        r = lax.rsqrt(jnp.mean(hs * hs, axis=-1, keepdims=True) + EPS)
        y = hs * r * nw_ref[:, sl] * _sigmoid(o_ref[:, sl].astype(F32))
        y_sc[:, sl] = y.astype(BF16)
    ya = jnp.dot(y_sc[...], w_ref[...], preferred_element_type=F32)
    out_ref[...] = _sigmoid(ga_ref[...].astype(F32)) * ya


def _mlstm_out(hf, hb, proj, nw, w, dv, o_blk, ga_blk):
    rows, d_v = hf.shape
    d = w.shape[1]
    return pl.pallas_call(
        functools.partial(_mlstm_out_kernel, dv),
        grid=(rows // ROW_TILE,),
        in_specs=[
            pl.BlockSpec((ROW_TILE, d_v), lambda i: (i, 0)),
            pl.BlockSpec((ROW_TILE, d_v), lambda i: (i, 0)),
            pl.BlockSpec((ROW_TILE, d_v), lambda i: (i, o_blk)),
            pl.BlockSpec((ROW_TILE, d), lambda i: (i, ga_blk)),
            pl.BlockSpec((1, d_v), lambda i: (0, 0)),
            pl.BlockSpec((d_v, d), lambda i: (0, 0)),
        ],
        out_specs=pl.BlockSpec((ROW_TILE, d), lambda i: (i, 0)),
        out_shape=jax.ShapeDtypeStruct((rows, d), F32),
        scratch_shapes=[pltpu.VMEM((ROW_TILE, d_v), BF16)],
        compiler_params=_cparams("parallel"),
        name="mlstm_out",
    )(hf, hb, proj, proj, nw, w)


CONV_ROW_BLOCK = 128
CONV_COL_BLOCK = 256


def _conv_kernel(a_ref, g_ref, ap_ref, gp_ref, an_ref, gn_ref, cw_ref, cb_ref, lnw_ref, lnb_ref,
                 w_ref, bo_ref, gb_ref, ya_ref, out_ref, c_sc, conv_sc):
    i = pl.program_id(0)
    last = pl.num_programs(0) - 1

    def glu(a, g):
        return a.astype(F32) * _sigmoid(g.astype(F32))

    c_sc[HALO:HALO + ROW_TILE, :] = glu(a_ref[...], g_ref[...])
    c_sc[0:HALO, :] = jnp.where(i > 0, glu(ap_ref[...], gp_ref[...]), 0.0)
    c_sc[HALO + ROW_TILE:, :] = jnp.where(i < last, glu(an_ref[...], gn_ref[...]), 0.0)

    d = a_ref.shape[1]
    first = HALO - CONV_HALF
    span = CONV_ROW_BLOCK + 2 * HALO
    for rb in range(ROW_TILE // CONV_ROW_BLOCK):
        r0 = rb * CONV_ROW_BLOCK
        for cb in range(d // CONV_COL_BLOCK):
            cs = slice(cb * CONV_COL_BLOCK, (cb + 1) * CONV_COL_BLOCK)
            x = c_sc[r0:r0 + span, cs]
            acc = jnp.broadcast_to(cb_ref[:, cs], (CONV_ROW_BLOCK, CONV_COL_BLOCK))
            for sh in range(SUBLANES):
                xs = x if sh == 0 else pltpu.roll(x, span - sh, 0)
                for al in range((CONV_WIDTH + first) // SUBLANES + 1):
                    tap = al * SUBLANES + sh - first
                    if 0 <= tap < CONV_WIDTH:
                        acc = acc + cw_ref[tap:tap + 1, cs] * xs[al * SUBLANES:al * SUBLANES + CONV_ROW_BLOCK]
            conv_sc[r0:r0 + CONV_ROW_BLOCK, cs] = acc

    x = conv_sc[...]
    mu = jnp.mean(x, axis=-1, keepdims=True)
    xc = x - mu
    var = jnp.mean(xc * xc, axis=-1, keepdims=True)
    y = xc * lax.rsqrt(var + EPS) * lnw_ref[...] + lnb_ref[...]
    y = y * _sigmoid(y)
    yb = jnp.dot(y.astype(BF16), w_ref[...], preferred_element_type=F32) + bo_ref[...]
    out_ref[...] = ya_ref[...] + _sigmoid(gb_ref[...].astype(F32)) * yb


def _conv_merge(proj, ya, cw, cb, lnw, lnb, w, bo, a_blk, g_blk, gb_blk):
    rows = proj.shape[0]
    d = w.shape[0]
    n_tiles = rows // ROW_TILE
    per = ROW_TILE // HALO
    n_halo = rows // HALO
    a_col = a_blk
    g_col = g_blk

    def prev(i):
        return jnp.maximum(i * per - 1, 0)

    def nxt(i):
        return jnp.minimum((i + 1) * per, n_halo - 1)

    return pl.pallas_call(
        _conv_kernel,
        grid=(n_tiles,),
        in_specs=[
            pl.BlockSpec((ROW_TILE, d), lambda i: (i, a_col)),
            pl.BlockSpec((ROW_TILE, d), lambda i: (i, g_col)),
            pl.BlockSpec((HALO, d), lambda i: (prev(i), a_col)),
            pl.BlockSpec((HALO, d), lambda i: (prev(i), g_col)),
            pl.BlockSpec((HALO, d), lambda i: (nxt(i), a_col)),
            pl.BlockSpec((HALO, d), lambda i: (nxt(i), g_col)),
            pl.BlockSpec((CONV_WIDTH, d), lambda i: (0, 0)),
            pl.BlockSpec((1, d), lambda i: (0, 0)),
            pl.BlockSpec((1, d), lambda i: (0, 0)),
            pl.BlockSpec((1, d), lambda i: (0, 0)),
            pl.BlockSpec((d, d), lambda i: (0, 0)),
            pl.BlockSpec((1, d), lambda i: (0, 0)),
            pl.BlockSpec((ROW_TILE, d), lambda i: (i, gb_blk)),
            pl.BlockSpec((ROW_TILE, d), lambda i: (i, 0)),
        ],
        out_specs=pl.BlockSpec((ROW_TILE, d), lambda i: (i, 0)),
        out_shape=jax.ShapeDtypeStruct((rows, d), F32),
        scratch_shapes=[
            pltpu.VMEM((ROW_TILE + 2 * HALO, d), F32),
            pltpu.VMEM((ROW_TILE, d), F32),
        ],
        compiler_params=_cparams("parallel"),
        name="conv_merge",
    )(proj, proj, proj, proj, proj, proj, cw, cb, lnw, lnb, w, bo, proj, ya)


def _wo_kernel(m_ref, h_ref, w_ref, nw_ref, h2_ref, ut_ref):
    h2 = h_ref[...] + jnp.dot(m_ref[...].astype(BF16), w_ref[...], preferred_element_type=F32)
    h2_ref[...] = h2
    r = lax.rsqrt(jnp.mean(h2 * h2, axis=-1, keepdims=True) + EPS)
    u = h2 * r * nw_ref[...]
    ut_ref[...] = u.T.astype(BF16)


def _wo_residual(merged, hp, w, nw):
    rows, d = hp.shape
    return pl.pallas_call(
        _wo_kernel,
        grid=(rows // ROW_TILE,),
        in_specs=[
            pl.BlockSpec((ROW_TILE, d), lambda i: (i, 0)),
            pl.BlockSpec((ROW_TILE, d), lambda i: (i, 0)),
            pl.BlockSpec((d, d), lambda i: (0, 0)),
            pl.BlockSpec((1, d), lambda i: (0, 0)),
        ],
        out_specs=[
            pl.BlockSpec((ROW_TILE, d), lambda i: (i, 0)),
            pl.BlockSpec((d, ROW_TILE), lambda i: (0, i)),
        ],
        out_shape=[
            jax.ShapeDtypeStruct((rows, d), F32),
            jax.ShapeDtypeStruct((d, rows), BF16),
        ],
        compiler_params=_cparams("parallel"),
        name="wo_residual",
    )(merged, hp, w, nw)


def _top_values(x, k):
    vals = []
    for _ in range(k):
        mx = jnp.max(x, axis=0, keepdims=True)
        vals.append(mx)
        x = jnp.where(x == mx, -jnp.inf, x)
    return jnp.concatenate(vals, axis=0)


def _odd_even_merge_sort_pairs(n):
    pairs = []

    def merge(lo, size, r):
        step = 2 * r
        if step < size:
            merge(lo, size, step)
            merge(lo + r, size, step)
            pairs.extend((i, i + r) for i in range(lo + r, lo + size - r, step))
        else:
            pairs.append((lo, lo + r))

    def sort(lo, size):
        if size > 1:
            sort(lo, size // 2)
            sort(lo + size // 2, size // 2)
            merge(lo, size, 1)

    sort(0, n)
    return pairs


def _compare_exchange(v, i, j):
    v[i], v[j] = jnp.maximum(v[i], v[j]), jnp.minimum(v[i], v[j])


def _top_values_network(x, k):
    assert x.shape[0] == SUBLANES * k and k & (k - 1) == 0
    v = [x[r * SUBLANES:(r + 1) * SUBLANES, :] for r in range(k)]
    for i, j in _odd_even_merge_sort_pairs(k):
        _compare_exchange(v, i, j)
    shift = SUBLANES // 2
    while shift:
        v = [jnp.maximum(v[i], pltpu.roll(v[k - 1 - i], shift, 0)) for i in range(k)]
        stride = k // 2
        while stride:
            for i in range(k):
                if i & stride == 0:
                    _compare_exchange(v, i, i + stride)
            stride //= 2
        shift //= 2
    return jnp.concatenate([b[0:1, :] for b in v], axis=0)


def _peer_route_kernel(n_keys, ut_ref, wq_ref, keys_ref, t1_ref, e1_ref, s2_ref, e2_ref):
    k = PEER_TOPK
    ut = ut_ref[...]
    tokens = ut.shape[1]
    for h in range(PEER_HEADS):
        tops = []
        scores = []
        for p in range(2):
            r0 = (2 * h + p) * n_keys
            q = jnp.dot(wq_ref[r0:r0 + n_keys, :], ut, preferred_element_type=F32)
            s = jnp.dot(keys_ref[p], q.astype(BF16), preferred_element_type=F32)
            scores.append(s)
            tops.append(_top_values_network(s, k))
        a1, a2 = tops
        n_y = [k // (x + 1) for x in range(k)]
        cand_rows = [a1[x:x + 1, :] + a2[0:n_y[x], :] for x in range(k)]
        fill = (-sum(n_y)) % SUBLANES
        if fill:
            cand_rows.append(jnp.full((fill, tokens), -jnp.inf, F32))
        top = _top_values(jnp.concatenate(cand_rows, axis=0), k)
        cmax = top[0:1, :]
        theta = top[k - 1:k, :]
        z = jnp.sum(jnp.exp(top - cmax), axis=0, keepdims=True)
        tau = jnp.full(scores[0].shape, jnp.inf, F32)
        for x in range(k):
            tau_x = jnp.min(jnp.where(cand_rows[x] >= theta, a2[0:n_y[x], :], jnp.inf), axis=0, keepdims=True)
            tau = jnp.where(scores[0] == a1[x:x + 1, :], tau_x, tau)
        t1_ref[h] = tau
        s2_ref[h] = scores[1]
        e1_ref[h] = jnp.exp(scores[0] - a1[0:1, :]) / z
        e2_ref[h] = jnp.exp(scores[1] - a2[0:1, :])


def _peer_route(ut, wq_t, keys):
    d, rows = ut.shape
    n_keys = keys.shape[1]
    tm = PEER_TOKEN_TILE
    big = pl.BlockSpec((PEER_HEADS, n_keys, tm), lambda t: (0, 0, t))
    big_shape = jax.ShapeDtypeStruct((PEER_HEADS, n_keys, rows), F32)
    return pl.pallas_call(
        functools.partial(_peer_route_kernel, n_keys),
        grid=(rows // tm,),
        in_specs=[
            pl.BlockSpec((d, tm), lambda t: (0, t)),
            pl.BlockSpec(wq_t.shape, lambda t: (0, 0)),
            pl.BlockSpec(keys.shape, lambda t: (0, 0, 0)),
        ],
        out_specs=[big, big, big, big],
        out_shape=[big_shape, big_shape, big_shape, big_shape],
        compiler_params=_cparams("parallel"),
        name="peer_route",
    )(ut, wq_t, keys)


PEER_GATE_ROWS = 64


def _peer_part(n_keys, part, tile, ut_ref, u_ref, vt_ref, gate_refs, out_ref, z_sc, act_new, act_old):
    t1_ref, e1_ref, s2_ref, e2_ref = gate_refs
    te, tm = PEER_EXPERT_TILE, PEER_TOKEN_TILE
    d = ut_ref.shape[0]
    gr = PEER_GATE_ROWS
    per = te // n_keys
    out_rows = d // per
    for il in range(per):
        i = tile * per + il
        c0 = il * n_keys
        z_sc[c0:c0 + n_keys, :] = jnp.dot(u_ref[part * te + c0:part * te + c0 + n_keys, :], ut_ref[...],
                                          preferred_element_type=F32)
        for jb in range(n_keys // gr):
            js = slice(jb * gr, (jb + 1) * gr)
            rows = slice(c0 + jb * gr, c0 + (jb + 1) * gr)
            for lc in range(tm // LANES):
                cols = slice(lc * LANES, (lc + 1) * LANES)
                gate = None
                for h in range(PEER_HEADS):
                    sel = jnp.where(s2_ref[h, js, cols] >= t1_ref[h, pl.ds(i, 1), :][:, cols],
                                    e2_ref[h, js, cols], 0.0)
                    term = sel * e1_ref[h, pl.ds(i, 1), :][:, cols]
                    gate = term if gate is None else gate + term
                act_new[rows, cols] = (gate * _gelu_tanh(z_sc[rows, cols])).astype(BF16)
        o0 = il * out_rows
        out_ref[o0:o0 + out_rows, :] += jnp.dot(vt_ref[o0:o0 + out_rows, :], act_old[...],
                                                preferred_element_type=F32)


def _peer_main_kernel(n_keys, n_pairs, ut_ref, u_ref, vta_ref, vtb_ref, vtl_ref, t1_ref, e1_ref, s2_ref,
                      e2_ref, out_ref, z_sc, act0_sc, act1_sc):
    s = pl.program_id(1)

    @pl.when(s == 0)
    def _():
        out_ref[...] = jnp.zeros_like(out_ref)
        act1_sc[...] = jnp.zeros_like(act1_sc)

    gate_refs = (t1_ref, e1_ref, s2_ref, e2_ref)
    _peer_part(n_keys, 0, 2 * s, ut_ref, u_ref, vta_ref, gate_refs, out_ref, z_sc, act0_sc, act1_sc)
    _peer_part(n_keys, 1, 2 * s + 1, ut_ref, u_ref, vtb_ref, gate_refs, out_ref, z_sc, act1_sc, act0_sc)

    @pl.when(s == n_pairs - 1)
    def _():
        out_ref[...] += jnp.dot(vtl_ref[...], act1_sc[...], preferred_element_type=F32)


def _peer_main(ut, u_w, v_t, t1, e1, s2, e2):
    d, rows = ut.shape
    n_exp = u_w.shape[0]
    n_keys = t1.shape[1]
    tm, te = PEER_TOKEN_TILE, PEER_EXPERT_TILE
    n_pairs = n_exp // (2 * te)
    big = pl.BlockSpec((PEER_HEADS, n_keys, tm), lambda t, s: (0, 0, t))
    return pl.pallas_call(
        functools.partial(_peer_main_kernel, n_keys, n_pairs),
        grid=(rows // tm, n_pairs),
        in_specs=[
            pl.BlockSpec((d, tm), lambda t, s: (0, t)),
            pl.BlockSpec((2 * te, d), lambda t, s: (s, 0)),
            pl.BlockSpec((d, te), lambda t, s: (0, jnp.maximum(2 * s - 1, 0))),
            pl.BlockSpec((d, te), lambda t, s: (0, 2 * s)),
            pl.BlockSpec((d, te), lambda t, s: (0, 2 * n_pairs - 1)),
            big, big, big, big,
        ],
        out_specs=pl.BlockSpec((d, tm), lambda t, s: (0, t)),
        out_shape=jax.ShapeDtypeStruct((d, rows), F32),
        scratch_shapes=[
            pltpu.VMEM((te, tm), F32),
            pltpu.VMEM((te, tm), BF16),
            pltpu.VMEM((te, tm), BF16),
        ],
        compiler_params=_cparams("parallel", "arbitrary"),
        name="peer_main",
    )(ut, u_w, v_t, v_t, v_t, t1, e1, s2, e2)


def _final_kernel(pt_ref, h2_ref, nw_ref, out_ref):
    h3 = h2_ref[...] + pt_ref[...].T
    r = lax.rsqrt(jnp.mean(h3 * h3, axis=-1, keepdims=True) + EPS)
    out_ref[0] = h3 * r * nw_ref[...]


def _final(peer_t, h2, nw, batch, seq, lp):
    d = h2.shape[1]
    tiles_per_batch = lp // OUT_TILE
    skip = tiles_per_batch - seq // OUT_TILE

    def pos(b, i):
        return b * tiles_per_batch + skip + i

    return pl.pallas_call(
        _final_kernel,
        grid=(batch, seq // OUT_TILE),
        in_specs=[
            pl.BlockSpec((d, OUT_TILE), lambda b, i: (0, pos(b, i))),
            pl.BlockSpec((OUT_TILE, d), lambda b, i: (pos(b, i), 0)),
            pl.BlockSpec((1, d), lambda b, i: (0, 0)),
        ],
        out_specs=pl.BlockSpec((1, OUT_TILE, d), lambda b, i: (b, i, 0)),
        out_shape=jax.ShapeDtypeStruct((batch, seq, d), F32),
        compiler_params=_cparams("parallel", "parallel"),
        name="final_norm",
    )(peer_t, h2, nw)


def _layer(hp, mask, batch, lp, pad, norm1_w, w_in, b_in, mlstm_norm_w, w_mlstm_out, conv_w, conv_b,
           conv_ln_w, conv_ln_b, w_conv_out, b_conv_out, w_o, norm2_w, peer_w_q, peer_sub_keys,
           peer_u, peer_v):
    d = hp.shape[1]
    d_v = w_mlstm_out.shape[0]
    dv = d_v // MLSTM_HEADS
    d_conv = w_conv_out.shape[0]
    d_qk = (w_in.shape[1] - 2 * d_v - FORGET_GATE_COLS - 2 * d_conv - 2 * d) // 2
    dk = d_qk // MLSTM_HEADS
    assert d_v == d and d_conv == d and 2 * d_qk == d_v, "column-block indexing assumes these widths"

    g0 = 2 * d_qk + 2 * d_v
    g1 = g0 + FORGET_GATE_COLS
    wm = jnp.concatenate([w_in[:, :g0], w_in[:, g1:]], axis=1).astype(BF16)
    bm = jnp.concatenate([b_in[:g0], b_in[g1:]])[None, :]
    wg = jnp.pad(w_in[:, g0:g1], ((0, 0), (0, LANES - FORGET_GATE_COLS))).astype(BF16)
    bg = jnp.pad(b_in[g0:g1], (0, LANES - FORGET_GATE_COLS))[None, :]
    o_blk, a_blk, g_blk, ga_blk, gb_blk = 2, 3, 4, 5, 6

    proj, gates = _inproj(hp, mask, norm1_w[None, :], wm, bm, wg, bg)
    gates_t = gates[:, :FORGET_GATE_COLS].T
    hf, hb = _mlstm(proj, gates, gates_t, batch, lp, pad, dk, dv)
    ya = _mlstm_out(hf, hb, proj, mlstm_norm_w[None, :], w_mlstm_out.astype(BF16), dv, o_blk, ga_blk)
    merged = _conv_merge(proj, ya, conv_w, conv_b[None, :], conv_ln_w[None, :], conv_ln_b[None, :],
                         w_conv_out.astype(BF16), b_conv_out[None, :], a_blk, g_blk, gb_blk)
    h2, ut = _wo_residual(merged, hp, w_o.astype(BF16), norm2_w[None, :])
    t1, e1, s2, e2 = _peer_route(ut, peer_w_q.T.astype(BF16), peer_sub_keys.astype(BF16))
    peer_t = _peer_main(ut, peer_u.astype(BF16), peer_v.T.astype(BF16), t1, e1, s2, e2)
    return h2, peer_t


def kernel(x, meta_tokens, norm1_w, w_in, b_in, mlstm_norm_w, w_mlstm_out, conv_w, conv_b, conv_ln_w,
           conv_ln_b, w_conv_out, b_conv_out, w_o, norm2_w, peer_w_q, peer_sub_keys, peer_u, peer_v,
           final_norm_w):
    batch, seq, d = x.shape
    depth = norm1_w.shape[0]
    assert depth == 1, "the fused final norm assumes a single layer"
    assert seq % OUT_TILE == 0
    seq_len = N_META + seq
    pad = HALO + (-(seq_len + HALO)) % CHUNK
    lp = pad + seq_len
    rows = batch * lp
    assert rows % ROW_TILE == 0 and rows % PEER_TOKEN_TILE == 0

    meta = jnp.broadcast_to(meta_tokens[None].astype(x.dtype), (batch, N_META, d))
    hp = jnp.concatenate([jnp.zeros((batch, pad, d), x.dtype), meta, x], axis=1).reshape(rows, d)
    mask = jnp.tile((jnp.arange(lp) >= pad).astype(F32), batch)[:, None]

    h2, peer_t = _layer(hp, mask, batch, lp, pad, norm1_w[0], w_in[0], b_in[0], mlstm_norm_w[0],
                        w_mlstm_out[0], conv_w[0], conv_b[0], conv_ln_w[0], conv_ln_b[0],
                        w_conv_out[0], b_conv_out[0], w_o[0], norm2_w[0], peer_w_q[0],
                        peer_sub_keys[0], peer_u[0], peer_v[0])
    return _final(peer_t, h2, final_norm_w[None, :], batch, seq, lp)
```

```python
import functools
import math

import jax
import jax.numpy as jnp
from jax import lax
from jax.experimental import pallas as pl
from jax.experimental.pallas import tpu as pltpu

F32 = jnp.float32
BF16 = jnp.bfloat16

LANES = 128
SUBLANES = 8
BF16_ROWS = 16
VMEM_LIMIT_BYTES = 56 * 1024 * 1024

N_META = 16
MLSTM_HEADS = 8
CONV_WIDTH = 31
CONV_HALF = CONV_WIDTH // 2
PEER_HEADS = 8
PEER_TOPK = 16
FORGET_GATE_COLS = 4 * MLSTM_HEADS
LOG_ZERO_GATE = -1.0e4
EPS = 1e-6
NEG_BIG = -1.0e30

ROW_TILE = 512
COL_TILE = 2048
CHUNK = 384
HALO = 16
PEER_TOKEN_TILE = 512
PEER_EXPERT_TILE = 512
OUT_TILE = 128


def _cparams(*sem):
    return pltpu.CompilerParams(dimension_semantics=sem, vmem_limit_bytes=VMEM_LIMIT_BYTES)


def _sigmoid(x):
    return 1.0 / (1.0 + jnp.exp(-x))


def _log_sigmoid(x):
    return jnp.minimum(x, 0.0) - jnp.log(1.0 + jnp.exp(-jnp.abs(x)))


def _gelu_tanh(x):
    k = -2.0 * math.sqrt(2.0 / math.pi) * math.log2(math.e)
    return x / (1.0 + jnp.exp2(x * (k + (k * 0.044715) * (x * x))))


def _inproj_kernel(h_ref, mask_ref, nw_ref, wm_ref, bm_ref, wg_ref, bg_ref, proj_ref, gates_ref, u_sc):
    @pl.when(pl.program_id(1) == 0)
    def _():
        x = h_ref[...]
        r = lax.rsqrt(jnp.mean(x * x, axis=-1, keepdims=True) + EPS)
        u = (x * r * nw_ref[...]).astype(BF16)
        u_sc[...] = u
        g = jnp.dot(u, wg_ref[...], preferred_element_type=F32) + bg_ref[...]
        gates_ref[...] = g * mask_ref[...]

    acc = jnp.dot(u_sc[...], wm_ref[...], preferred_element_type=F32) + bm_ref[...]
    proj_ref[...] = (acc * mask_ref[...]).astype(BF16)


def _inproj(hp, mask, nw, wm, bm, wg, bg):
    rows, d = hp.shape
    n_main = wm.shape[1]
    grid = (rows // ROW_TILE, n_main // COL_TILE)
    return pl.pallas_call(
        _inproj_kernel,
        grid=grid,
        in_specs=[
            pl.BlockSpec((ROW_TILE, d), lambda i, j: (i, 0)),
            pl.BlockSpec((ROW_TILE, 1), lambda i, j: (i, 0)),
            pl.BlockSpec((1, d), lambda i, j: (0, 0)),
            pl.BlockSpec((d, COL_TILE), lambda i, j: (0, j)),
            pl.BlockSpec((1, COL_TILE), lambda i, j: (0, j)),
            pl.BlockSpec((d, LANES), lambda i, j: (0, 0)),
            pl.BlockSpec((1, LANES), lambda i, j: (0, 0)),
        ],
        out_specs=[
            pl.BlockSpec((ROW_TILE, COL_TILE), lambda i, j: (i, j)),
            pl.BlockSpec((ROW_TILE, LANES), lambda i, j: (i, 0)),
        ],
        out_shape=[
            jax.ShapeDtypeStruct((rows, n_main), BF16),
            jax.ShapeDtypeStruct((rows, LANES), F32),
        ],
        scratch_shapes=[pltpu.VMEM((ROW_TILE, d), BF16)],
        compiler_params=_cparams("parallel", "arbitrary"),
        name="inproj",
    )(hp, mask, nw, wm, bm, wg, bg)


def _mlstm_kernel(pad, n_chunks, dk, dv,
                  qf_ref, kf_ref, vf_ref, gf_ref, gtf_ref,
                  qb_ref, kb_ref, vb_ref, gb_ref, gtb_ref,
                  hf_ref, hb_ref, c_sc, n_sc, m_sc):
    c = pl.program_id(1)

    @pl.when(c == 0)
    def _():
        c_sc[...] = jnp.zeros_like(c_sc)
        n_sc[...] = jnp.zeros_like(n_sc)
        m_sc[...] = jnp.zeros_like(m_sc)

    scale = dk ** -0.5
    row = lax.broadcasted_iota(jnp.int32, (CHUNK, CHUNK), 0)
    col = lax.broadcasted_iota(jnp.int32, (CHUNK, CHUNK), 1)
    lower = col <= row
    upper = col >= row
    row1 = lax.broadcasted_iota(jnp.int32, (CHUNK, 1), 0)
    col1 = lax.broadcasted_iota(jnp.int32, (1, CHUNK), 1)

    dirs = (
        (qf_ref, kf_ref, vf_ref, gf_ref, gtf_ref, hf_ref, c, lower, upper),
        (qb_ref, kb_ref, vb_ref, gb_ref, gtb_ref, hb_ref, n_chunks - 1 - c, upper, lower),
    )
    for d, (q_ref, k_ref, v_ref, g_ref, gt_ref, out_ref, chunk, mask, mask_t) in enumerate(dirs):
        base = chunk * CHUNK
        valid_c = (base + row1) >= pad
        valid_r = (base + col1) >= pad
        g0 = 2 * MLSTM_HEADS * d
        gc = g_ref[...]
        gr = gt_ref[...]
        li_c_all = jnp.where(valid_c, gc[:, g0:g0 + MLSTM_HEADS], LOG_ZERO_GATE)
        lf_c_all = jnp.where(valid_c, _log_sigmoid(gc[:, g0 + MLSTM_HEADS:g0 + 2 * MLSTM_HEADS]), 0.0)
        li_r_all = jnp.where(valid_r, gr[g0:g0 + MLSTM_HEADS, :], LOG_ZERO_GATE)
        lf_r_all = jnp.where(valid_r, _log_sigmoid(gr[g0 + MLSTM_HEADS:g0 + 2 * MLSTM_HEADS, :]), 0.0)
        b_c_all = jnp.dot(mask.astype(F32), lf_c_all, precision=lax.Precision.HIGHEST,
                          preferred_element_type=F32)
        b_r_all = jnp.dot(lf_r_all, mask_t.astype(F32), precision=lax.Precision.HIGHEST,
                          preferred_element_type=F32)
        g_all = jnp.sum(lf_c_all, axis=0, keepdims=True)

        for h in range(MLSTM_HEADS):
            idx = d * MLSTM_HEADS + h
            li_c = li_c_all[:, h:h + 1]
            b_c = b_c_all[:, h:h + 1]
            li_r = li_r_all[h:h + 1, :]
            b_r = b_r_all[h:h + 1, :]
            g = g_all[:, h:h + 1]
            m_prev = m_sc[idx, 0:1, 0:1]
            n_prev = n_sc[idx, 0:1, :]
            ct = c_sc[idx]

            qh = q_ref[:, h * dk:(h + 1) * dk]
            kh = k_ref[:, h * dk:(h + 1) * dk]
            vh = v_ref[:, h * dv:(h + 1) * dv]
            k32 = kh.astype(F32)

            a_c = g - b_c + li_c
            m_new = jnp.maximum(g + m_prev, jnp.max(a_c, axis=0, keepdims=True))
            decay = jnp.exp(g + m_prev - m_new)
            w_c = jnp.exp(a_c - m_new)

            inter = b_c + m_prev
            logd = jnp.where(mask, b_c - b_r + li_r, NEG_BIG)
            m_c = jnp.maximum(inter, jnp.max(logd, axis=1, keepdims=True))
            p = jnp.exp(logd - m_c)
            qk = lax.dot_general(qh, kh, (((1,), (1,)), ((), ())), preferred_element_type=F32)
            s = qk * (p * scale)
            w_inter = jnp.exp(inter - m_c) * scale
            qc = jnp.dot(qh, ct.astype(BF16), preferred_element_type=F32)
            num = jnp.dot(s.astype(BF16), vh, preferred_element_type=F32) + w_inter * qc
            qn = jnp.sum(qh.astype(F32) * n_prev, axis=1, keepdims=True)
            den = jnp.sum(s, axis=1, keepdims=True) + w_inter * qn
            out_ref[:, h * dv:(h + 1) * dv] = num / jnp.maximum(jnp.abs(den), jnp.exp(-m_c))

            wv = (w_c * vh.astype(F32)).astype(BF16)
            kt = k32.T.astype(BF16)
            c_sc[idx] = decay * ct + jnp.dot(kt, wv, preferred_element_type=F32)
            n_new = decay * n_prev + jnp.sum(w_c * k32, axis=0, keepdims=True)
            n_sc[idx] = jnp.broadcast_to(n_new, (SUBLANES, dk))
            m_sc[idx] = jnp.broadcast_to(m_new, (SUBLANES, LANES))


def _mlstm(proj, gates, gates_t, batch, lp, pad, dk, dv):
    rows = proj.shape[0]
    n_chunks = lp // CHUNK
    d_qk = MLSTM_HEADS * dk
    d_v = MLSTM_HEADS * dv
    v_blk = (2 * d_qk) // d_v

    def fwd(b, c):
        return b * n_chunks + c

    def bwd(b, c):
        return b * n_chunks + n_chunks - 1 - c

    def specs(pos):
        return [
            pl.BlockSpec((CHUNK, d_qk), lambda b, c: (pos(b, c), 0)),
            pl.BlockSpec((CHUNK, d_qk), lambda b, c: (pos(b, c), 1)),
            pl.BlockSpec((CHUNK, d_v), lambda b, c: (pos(b, c), v_blk)),
            pl.BlockSpec((CHUNK, LANES), lambda b, c: (pos(b, c), 0)),
            pl.BlockSpec((FORGET_GATE_COLS, CHUNK), lambda b, c: (0, pos(b, c))),
        ]

    kern = functools.partial(_mlstm_kernel, pad, n_chunks, dk, dv)
    n_state = 2 * MLSTM_HEADS
    return pl.pallas_call(
        kern,
        grid=(batch, n_chunks),
        in_specs=specs(fwd) + specs(bwd),
        out_specs=[
            pl.BlockSpec((CHUNK, d_v), lambda b, c: (fwd(b, c), 0)),
            pl.BlockSpec((CHUNK, d_v), lambda b, c: (bwd(b, c), 0)),
        ],
        out_shape=[jax.ShapeDtypeStruct((rows, d_v), F32)] * 2,
        scratch_shapes=[
            pltpu.VMEM((n_state, dk, dv), F32),
            pltpu.VMEM((n_state, SUBLANES, dk), F32),
            pltpu.VMEM((n_state, SUBLANES, LANES), F32),
        ],
        compiler_params=_cparams("parallel", "arbitrary"),
        name="mlstm",
    )(proj, proj, proj, gates, gates_t, proj, proj, proj, gates, gates_t)


def _mlstm_out_kernel(dv, hf_ref, hb_ref, o_ref, ga_ref, nw_ref, w_ref, out_ref, y_sc):
    for h in range(MLSTM_HEADS):
        sl = slice(h * dv, (h + 1) * dv)
        hs = hf_ref[:, sl] + hb_ref[:, sl]
        r = lax.rsqrt(jnp.mean(hs * hs, axis=-1, keepdims=True) + EPS)
        y = hs * r * nw_ref[:, sl] * _sigmoid(o_ref[:, sl].astype(F32))
        y_sc[:, sl] = y.astype(BF16)
    ya = jnp.dot(y_sc[...], w_ref[...], preferred_element_type=F32)
    out_ref[...] = _sigmoid(ga_ref[...].astype(F32)) * ya


def _mlstm_out(hf, hb, proj, nw, w, dv, o_blk, ga_blk):
    rows, d_v = hf.shape
    d = w.shape[1]
    return pl.pallas_call(
        functools.partial(_mlstm_out_kernel, dv),
        grid=(rows // ROW_TILE,),
        in_specs=[
            pl.BlockSpec((ROW_TILE, d_v), lambda i: (i, 0)),
            pl.BlockSpec((ROW_TILE, d_v), lambda i: (i, 0)),
            pl.BlockSpec((ROW_TILE, d_v), lambda i: (i, o_blk)),
            pl.BlockSpec((ROW_TILE, d), lambda i: (i, ga_blk)),
            pl.BlockSpec((1, d_v), lambda i: (0, 0)),
            pl.BlockSpec((d_v, d), lambda i: (0, 0)),
        ],
        out_specs=pl.BlockSpec((ROW_TILE, d), lambda i: (i, 0)),
        out_shape=jax.ShapeDtypeStruct((rows, d), F32),
        scratch_shapes=[pltpu.VMEM((ROW_TILE, d_v), BF16)],
        compiler_params=_cparams("parallel"),
        name="mlstm_out",
    )(hf, hb, proj, proj, nw, w)


CONV_ROW_BLOCK = 256
CONV_COL_BLOCK = 256


def _conv_kernel(a_ref, g_ref, ap_ref, gp_ref, an_ref, gn_ref, cw_ref, cb_ref, lnw_ref, lnb_ref,
                 w_ref, bo_ref, gb_ref, ya_ref, out_ref, c_sc, conv_sc):
    i = pl.program_id(0)
    last = pl.num_programs(0) - 1

    def glu(a, g):
        return a.astype(F32) * _sigmoid(g.astype(F32))

    c_sc[HALO:HALO + ROW_TILE, :] = glu(a_ref[...], g_ref[...])
    c_sc[0:HALO, :] = jnp.where(i > 0, glu(ap_ref[...], gp_ref[...]), 0.0)
    c_sc[HALO + ROW_TILE:, :] = jnp.where(i < last, glu(an_ref[...], gn_ref[...]), 0.0)

    d = a_ref.shape[1]
    first = HALO - CONV_HALF
    span = CONV_ROW_BLOCK + 2 * HALO
    for rb in range(ROW_TILE // CONV_ROW_BLOCK):
        r0 = rb * CONV_ROW_BLOCK
        for cb in range(d // CONV_COL_BLOCK):
            cs = slice(cb * CONV_COL_BLOCK, (cb + 1) * CONV_COL_BLOCK)
            x = c_sc[r0:r0 + span, cs]
            acc = jnp.broadcast_to(cb_ref[:, cs], (CONV_ROW_BLOCK, CONV_COL_BLOCK))
            for sh in range(SUBLANES):
                xs = x if sh == 0 else pltpu.roll(x, span - sh, 0)
                for al in range((CONV_WIDTH + first) // SUBLANES + 1):
                    tap = al * SUBLANES + sh - first
                    if 0 <= tap < CONV_WIDTH:
                        acc = acc + cw_ref[tap:tap + 1, cs] * xs[al * SUBLANES:al * SUBLANES + CONV_ROW_BLOCK]
            conv_sc[r0:r0 + CONV_ROW_BLOCK, cs] = acc

    x = conv_sc[...]
    mu = jnp.mean(x, axis=-1, keepdims=True)
    xc = x - mu
    var = jnp.mean(xc * xc, axis=-1, keepdims=True)
    y = xc * lax.rsqrt(var + EPS) * lnw_ref[...] + lnb_ref[...]
    y = y * _sigmoid(y)
    yb = jnp.dot(y.astype(BF16), w_ref[...], preferred_element_type=F32) + bo_ref[...]
    out_ref[...] = ya_ref[...] + _sigmoid(gb_ref[...].astype(F32)) * yb


def _conv_merge(proj, ya, cw, cb, lnw, lnb, w, bo, a_blk, g_blk, gb_blk):
    rows = proj.shape[0]
    d = w.shape[0]
    n_tiles = rows // ROW_TILE
    per = ROW_TILE // HALO
    n_halo = rows // HALO
    a_col = a_blk
    g_col = g_blk

    def prev(i):
        return jnp.maximum(i * per - 1, 0)

    def nxt(i):
        return jnp.minimum((i + 1) * per, n_halo - 1)

    return pl.pallas_call(
        _conv_kernel,
        grid=(n_tiles,),
        in_specs=[
            pl.BlockSpec((ROW_TILE, d), lambda i: (i, a_col)),
            pl.BlockSpec((ROW_TILE, d), lambda i: (i, g_col)),
            pl.BlockSpec((HALO, d), lambda i: (prev(i), a_col)),
            pl.BlockSpec((HALO, d), lambda i: (prev(i), g_col)),
            pl.BlockSpec((HALO, d), lambda i: (nxt(i), a_col)),
            pl.BlockSpec((HALO, d), lambda i: (nxt(i), g_col)),
            pl.BlockSpec((CONV_WIDTH, d), lambda i: (0, 0)),
            pl.BlockSpec((1, d), lambda i: (0, 0)),
            pl.BlockSpec((1, d), lambda i: (0, 0)),
            pl.BlockSpec((1, d), lambda i: (0, 0)),
            pl.BlockSpec((d, d), lambda i: (0, 0)),
            pl.BlockSpec((1, d), lambda i: (0, 0)),
            pl.BlockSpec((ROW_TILE, d), lambda i: (i, gb_blk)),
            pl.BlockSpec((ROW_TILE, d), lambda i: (i, 0)),
        ],
        out_specs=pl.BlockSpec((ROW_TILE, d), lambda i: (i, 0)),
        out_shape=jax.ShapeDtypeStruct((rows, d), F32),
        scratch_shapes=[
            pltpu.VMEM((ROW_TILE + 2 * HALO, d), F32),
            pltpu.VMEM((ROW_TILE, d), F32),
        ],
        compiler_params=_cparams("parallel"),
        name="conv_merge",
    )(proj, proj, proj, proj, proj, proj, cw, cb, lnw, lnb, w, bo, proj, ya)


def _wo_kernel(m_ref, h_ref, w_ref, nw_ref, h2_ref, ut_ref):
    h2 = h_ref[...] + jnp.dot(m_ref[...].astype(BF16), w_ref[...], preferred_element_type=F32)
    h2_ref[...] = h2
    r = lax.rsqrt(jnp.mean(h2 * h2, axis=-1, keepdims=True) + EPS)
    u = h2 * r * nw_ref[...]
    ut_ref[...] = u.T.astype(BF16)


def _wo_residual(merged, hp, w, nw):
    rows, d = hp.shape
    return pl.pallas_call(
        _wo_kernel,
        grid=(rows // ROW_TILE,),
        in_specs=[
            pl.BlockSpec((ROW_TILE, d), lambda i: (i, 0)),
            pl.BlockSpec((ROW_TILE, d), lambda i: (i, 0)),
            pl.BlockSpec((d, d), lambda i: (0, 0)),
            pl.BlockSpec((1, d), lambda i: (0, 0)),
        ],
        out_specs=[
            pl.BlockSpec((ROW_TILE, d), lambda i: (i, 0)),
            pl.BlockSpec((d, ROW_TILE), lambda i: (0, i)),
        ],
        out_shape=[
            jax.ShapeDtypeStruct((rows, d), F32),
            jax.ShapeDtypeStruct((d, rows), BF16),
        ],
        compiler_params=_cparams("parallel"),
        name="wo_residual",
    )(merged, hp, w, nw)


def _top_values(x, k):
    vals = []
    for _ in range(k):
        mx = jnp.max(x, axis=0, keepdims=True)
        vals.append(mx)
        x = jnp.where(x == mx, -jnp.inf, x)
    return jnp.concatenate(vals, axis=0)


def _odd_even_merge_sort_pairs(n):
    pairs = []

    def merge(lo, size, r):
        step = 2 * r
        if step < size:
            merge(lo, size, step)
            merge(lo + r, size, step)
            pairs.extend((i, i + r) for i in range(lo + r, lo + size - r, step))
        else:
            pairs.append((lo, lo + r))

    def sort(lo, size):
        if size > 1:
            sort(lo, size // 2)
            sort(lo + size // 2, size // 2)
            merge(lo, size, 1)

    sort(0, n)
    return pairs


def _compare_exchange(v, i, j):
    v[i], v[j] = jnp.maximum(v[i], v[j]), jnp.minimum(v[i], v[j])


def _top_values_network(x, k):
    assert x.shape[0] == SUBLANES * k and k & (k - 1) == 0
    v = [x[r * SUBLANES:(r + 1) * SUBLANES, :] for r in range(k)]
    for i, j in _odd_even_merge_sort_pairs(k):
        _compare_exchange(v, i, j)
    shift = SUBLANES // 2
    while shift:
        v = [jnp.maximum(v[i], pltpu.roll(v[k - 1 - i], shift, 0)) for i in range(k)]
        stride = k // 2
        while stride:
            for i in range(k):
                if i & stride == 0:
                    _compare_exchange(v, i, i + stride)
            stride //= 2
        shift //= 2
    return jnp.concatenate([b[0:1, :] for b in v], axis=0)


def _peer_route_kernel(n_keys, ut_ref, wq_ref, keys_ref, t1_ref, e1_ref, s2_ref, e2_ref):
    k = PEER_TOPK
    ut = ut_ref[...]
    tokens = ut.shape[1]
    for h in range(PEER_HEADS):
        tops = []
        scores = []
        for p in range(2):
            r0 = (2 * h + p) * n_keys
            q = jnp.dot(wq_ref[r0:r0 + n_keys, :], ut, preferred_element_type=F32)
            s = jnp.dot(keys_ref[p], q.astype(BF16), preferred_element_type=F32)
            scores.append(s)
            tops.append(_top_values_network(s, k))
        a1, a2 = tops
        n_y = [k // (x + 1) for x in range(k)]
        cand_rows = [a1[x:x + 1, :] + a2[0:n_y[x], :] for x in range(k)]
        fill = (-sum(n_y)) % SUBLANES
        if fill:
            cand_rows.append(jnp.full((fill, tokens), -jnp.inf, F32))
        top = _top_values(jnp.concatenate(cand_rows, axis=0), k)
        cmax = top[0:1, :]
        theta = top[k - 1:k, :]
        z = jnp.sum(jnp.exp(top - cmax), axis=0, keepdims=True)
        tau = jnp.full(scores[0].shape, jnp.inf, F32)
        for x in range(k):
            tau_x = jnp.min(jnp.where(cand_rows[x] >= theta, a2[0:n_y[x], :], jnp.inf), axis=0, keepdims=True)
            tau = jnp.where(scores[0] == a1[x:x + 1, :], tau_x, tau)
        t1_ref[h] = tau
        s2_ref[h] = scores[1]
        e1_ref[h] = jnp.exp(scores[0] - a1[0:1, :]) / z
        e2_ref[h] = jnp.exp(scores[1] - a2[0:1, :])


def _peer_route(ut, wq_t, keys):
    d, rows = ut.shape
    n_keys = keys.shape[1]
    tm = PEER_TOKEN_TILE
    big = pl.BlockSpec((PEER_HEADS, n_keys, tm), lambda t: (0, 0, t))
    big_shape = jax.ShapeDtypeStruct((PEER_HEADS, n_keys, rows), F32)
    return pl.pallas_call(
        functools.partial(_peer_route_kernel, n_keys),
        grid=(rows // tm,),
        in_specs=[
            pl.BlockSpec((d, tm), lambda t: (0, t)),
            pl.BlockSpec(wq_t.shape, lambda t: (0, 0)),
            pl.BlockSpec(keys.shape, lambda t: (0, 0, 0)),
        ],
        out_specs=[big, big, big, big],
        out_shape=[big_shape, big_shape, big_shape, big_shape],
        compiler_params=_cparams("parallel"),
        name="peer_route",
    )(ut, wq_t, keys)


PEER_GATE_ROWS = 128


def _peer_part(n_keys, part, tile, ut_ref, u_ref, vt_ref, gate_refs, out_ref, z_sc, act_new, act_old):
    t1_ref, e1_ref, s2_ref, e2_ref = gate_refs
    te, tm = PEER_EXPERT_TILE, PEER_TOKEN_TILE
    d = ut_ref.shape[0]
    gr = PEER_GATE_ROWS
    per = te // n_keys
    out_rows = d // per
    for il in range(per):
        i = tile * per + il
        c0 = il * n_keys
        z_sc[c0:c0 + n_keys, :] = jnp.dot(u_ref[part * te + c0:part * te + c0 + n_keys, :], ut_ref[...],
                                          preferred_element_type=F32)
        for jb in range(n_keys // gr):
            js = slice(jb * gr, (jb + 1) * gr)
            rows = slice(c0 + jb * gr, c0 + (jb + 1) * gr)
            for lc in range(tm // LANES):
                cols = slice(lc * LANES, (lc + 1) * LANES)
                gate = None
                for h in range(PEER_HEADS):
                    sel = jnp.where(s2_ref[h, js, cols] >= t1_ref[h, pl.ds(i, 1), :][:, cols],
                                    e2_ref[h, js, cols], 0.0)
                    term = sel * e1_ref[h, pl.ds(i, 1), :][:, cols]
                    gate = term if gate is None else gate + term
                act_new[rows, cols] = (gate * _gelu_tanh(z_sc[rows, cols])).astype(BF16)
        o0 = il * out_rows
        out_ref[o0:o0 + out_rows, :] += jnp.dot(vt_ref[o0:o0 + out_rows, :], act_old[...],
                                                preferred_element_type=F32)


def _peer_main_kernel(n_keys, n_pairs, ut_ref, u_ref, vta_ref, vtb_ref, vtl_ref, t1_ref, e1_ref, s2_ref,
                      e2_ref, out_ref, z_sc, act0_sc, act1_sc):
    s = pl.program_id(1)

    @pl.when(s == 0)
    def _():
        out_ref[...] = jnp.zeros_like(out_ref)
        act1_sc[...] = jnp.zeros_like(act1_sc)

    gate_refs = (t1_ref, e1_ref, s2_ref, e2_ref)
    _peer_part(n_keys, 0, 2 * s, ut_ref, u_ref, vta_ref, gate_refs, out_ref, z_sc, act0_sc, act1_sc)
    _peer_part(n_keys, 1, 2 * s + 1, ut_ref, u_ref, vtb_ref, gate_refs, out_ref, z_sc, act1_sc, act0_sc)

    @pl.when(s == n_pairs - 1)
    def _():
        out_ref[...] += jnp.dot(vtl_ref[...], act1_sc[...], preferred_element_type=F32)


def _peer_main(ut, u_w, v_t, t1, e1, s2, e2):
    d, rows = ut.shape
    n_exp = u_w.shape[0]
    n_keys = t1.shape[1]
    tm, te = PEER_TOKEN_TILE, PEER_EXPERT_TILE
    n_pairs = n_exp // (2 * te)
    big = pl.BlockSpec((PEER_HEADS, n_keys, tm), lambda t, s: (0, 0, t))
    return pl.pallas_call(
        functools.partial(_peer_main_kernel, n_keys, n_pairs),
        grid=(rows // tm, n_pairs),
        in_specs=[
            pl.BlockSpec((d, tm), lambda t, s: (0, t)),
            pl.BlockSpec((2 * te, d), lambda t, s: (s, 0)),
            pl.BlockSpec((d, te), lambda t, s: (0, jnp.maximum(2 * s - 1, 0))),
            pl.BlockSpec((d, te), lambda t, s: (0, 2 * s)),
            pl.BlockSpec((d, te), lambda t, s: (0, 2 * n_pairs - 1)),
            big, big, big, big,
        ],
        out_specs=pl.BlockSpec((d, tm), lambda t, s: (0, t)),
        out_shape=jax.ShapeDtypeStruct((d, rows), F32),
        scratch_shapes=[
            pltpu.VMEM((te, tm), F32),
            pltpu.VMEM((te, tm), BF16),
            pltpu.VMEM((te, tm), BF16),
        ],
        compiler_params=_cparams("parallel", "arbitrary"),
        name="peer_main",
    )(ut, u_w, v_t, v_t, v_t, t1, e1, s2, e2)


def _final_kernel(pt_ref, h2_ref, nw_ref, out_ref):
    h3 = h2_ref[...] + pt_ref[...].T
    r = lax.rsqrt(jnp.mean(h3 * h3, axis=-1, keepdims=True) + EPS)
    out_ref[0] = h3 * r * nw_ref[...]


def _final(peer_t, h2, nw, batch, seq, lp):
    d = h2.shape[1]
    tiles_per_batch = lp // OUT_TILE
    skip = tiles_per_batch - seq // OUT_TILE

    def pos(b, i):
        return b * tiles_per_batch + skip + i

    return pl.pallas_call(
        _final_kernel,
        grid=(batch, seq // OUT_TILE),
        in_specs=[
            pl.BlockSpec((d, OUT_TILE), lambda b, i: (0, pos(b, i))),
            pl.BlockSpec((OUT_TILE, d), lambda b, i: (pos(b, i), 0)),
            pl.BlockSpec((1, d), lambda b, i: (0, 0)),
        ],
        out_specs=pl.BlockSpec((1, OUT_TILE, d), lambda b, i: (b, i, 0)),
        out_shape=jax.ShapeDtypeStruct((batch, seq, d), F32),
        compiler_params=_cparams("parallel", "parallel"),
        name="final_norm",
    )(peer_t, h2, nw)


def _layer(hp, mask, batch, lp, pad, norm1_w, w_in, b_in, mlstm_norm_w, w_mlstm_out, conv_w, conv_b,
           conv_ln_w, conv_ln_b, w_conv_out, b_conv_out, w_o, norm2_w, peer_w_q, peer_sub_keys,
           peer_u, peer_v):
    d = hp.shape[1]
    d_v = w_mlstm_out.shape[0]
    dv = d_v // MLSTM_HEADS
    d_conv = w_conv_out.shape[0]
    d_qk = (w_in.shape[1] - 2 * d_v - FORGET_GATE_COLS - 2 * d_conv - 2 * d) // 2
    dk = d_qk // MLSTM_HEADS
    assert d_v == d and d_conv == d and 2 * d_qk == d_v, "column-block indexing assumes these widths"

    g0 = 2 * d_qk + 2 * d_v
    g1 = g0 + FORGET_GATE_COLS
    wm = jnp.concatenate([w_in[:, :g0], w_in[:, g1:]], axis=1).astype(BF16)
    bm = jnp.concatenate([b_in[:g0], b_in[g1:]])[None, :]
    wg = jnp.pad(w_in[:, g0:g1], ((0, 0), (0, LANES - FORGET_GATE_COLS))).astype(BF16)
    bg = jnp.pad(b_in[g0:g1], (0, LANES - FORGET_GATE_COLS))[None, :]
    o_blk, a_blk, g_blk, ga_blk, gb_blk = 2, 3, 4, 5, 6

    proj, gates = _inproj(hp, mask, norm1_w[None, :], wm, bm, wg, bg)
    gates_t = gates[:, :FORGET_GATE_COLS].T
    hf, hb = _mlstm(proj, gates, gates_t, batch, lp, pad, dk, dv)
    ya = _mlstm_out(hf, hb, proj, mlstm_norm_w[None, :], w_mlstm_out.astype(BF16), dv, o_blk, ga_blk)
    merged = _conv_merge(proj, ya, conv_w, conv_b[None, :], conv_ln_w[None, :], conv_ln_b[None, :],
                         w_conv_out.astype(BF16), b_conv_out[None, :], a_blk, g_blk, gb_blk)
    h2, ut = _wo_residual(merged, hp, w_o.astype(BF16), norm2_w[None, :])
    t1, e1, s2, e2 = _peer_route(ut, peer_w_q.T.astype(BF16), peer_sub_keys.astype(BF16))
    peer_t = _peer_main(ut, peer_u.astype(BF16), peer_v.T.astype(BF16), t1, e1, s2, e2)
    return h2, peer_t


def kernel(x, meta_tokens, norm1_w, w_in, b_in, mlstm_norm_w, w_mlstm_out, conv_w, conv_b, conv_ln_w,
           conv_ln_b, w_conv_out, b_conv_out, w_o, norm2_w, peer_w_q, peer_sub_keys, peer_u, peer_v,
           final_norm_w):
    batch, seq, d = x.shape
    depth = norm1_w.shape[0]
    assert depth == 1, "the fused final norm assumes a single layer"
    assert seq % OUT_TILE == 0
    seq_len = N_META + seq
    pad = HALO + (-(seq_len + HALO)) % CHUNK
    lp = pad + seq_len
    rows = batch * lp
    assert rows % ROW_TILE == 0 and rows % PEER_TOKEN_TILE == 0

    meta = jnp.broadcast_to(meta_tokens[None].astype(x.dtype), (batch, N_META, d))
    hp = jnp.concatenate([jnp.zeros((batch, pad, d), x.dtype), meta, x], axis=1).reshape(rows, d)
    mask = jnp.tile((jnp.arange(lp) >= pad).astype(F32), batch)[:, None]

    h2, peer_t = _layer(hp, mask, batch, lp, pad, norm1_w[0], w_in[0], b_in[0], mlstm_norm_w[0],
                        w_mlstm_out[0], conv_w[0], conv_b[0], conv_ln_w[0], conv_ln_b[0],
                        w_conv_out[0], b_conv_out[0], w_o[0], norm2_w[0], peer_w_q[0],
                        peer_sub_keys[0], peer_u[0], peer_v[0])
    return _final(peer_t, h2, final_norm_w[None, :], batch, seq, lp)
```

```python
import functools
import math

import jax
import jax.numpy as jnp
from jax import lax
from jax.experimental import pallas as pl
from jax.experimental.pallas import tpu as pltpu

F32 = jnp.float32
BF16 = jnp.bfloat16

LANES = 128
SUBLANES = 8
BF16_ROWS = 16
VMEM_LIMIT_BYTES = 56 * 1024 * 1024

N_META = 16
MLSTM_HEADS = 8
CONV_WIDTH = 31
CONV_HALF = CONV_WIDTH // 2
PEER_HEADS = 8
PEER_TOPK = 16
FORGET_GATE_COLS = 4 * MLSTM_HEADS
LOG_ZERO_GATE = -1.0e4
EPS = 1e-6
NEG_BIG = -1.0e30

ROW_TILE = 512
COL_TILE = 2048
CHUNK = 384
HALO = 16
PEER_TOKEN_TILE = 512
PEER_EXPERT_TILE = 512
OUT_TILE = 128


def _cparams(*sem):
    return pltpu.CompilerParams(dimension_semantics=sem, vmem_limit_bytes=VMEM_LIMIT_BYTES)


def _sigmoid(x):
    return 1.0 / (1.0 + jnp.exp(-x))


def _log_sigmoid(x):
    return jnp.minimum(x, 0.0) - jnp.log(1.0 + jnp.exp(-jnp.abs(x)))


def _gelu_tanh(x):
    k = -2.0 * math.sqrt(2.0 / math.pi) * math.log2(math.e)
    return x / (1.0 + jnp.exp2(x * (k + (k * 0.044715) * (x * x))))


def _inproj_kernel(h_ref, mask_ref, nw_ref, wm_ref, bm_ref, wg_ref, bg_ref, proj_ref, gates_ref, u_sc):
    @pl.when(pl.program_id(1) == 0)
    def _():
        x = h_ref[...]
        r = lax.rsqrt(jnp.mean(x * x, axis=-1, keepdims=True) + EPS)
        u = (x * r * nw_ref[...]).astype(BF16)
        u_sc[...] = u
        g = jnp.dot(u, wg_ref[...], preferred_element_type=F32) + bg_ref[...]
        gates_ref[...] = g * mask_ref[...]

    acc = jnp.dot(u_sc[...], wm_ref[...], preferred_element_type=F32) + bm_ref[...]
    proj_ref[...] = (acc * mask_ref[...]).astype(BF16)


def _inproj(hp, mask, nw, wm, bm, wg, bg):
    rows, d = hp.shape
    n_main = wm.shape[1]
    grid = (rows // ROW_TILE, n_main // COL_TILE)
    return pl.pallas_call(
        _inproj_kernel,
        grid=grid,
        in_specs=[
            pl.BlockSpec((ROW_TILE, d), lambda i, j: (i, 0)),
            pl.BlockSpec((ROW_TILE, 1), lambda i, j: (i, 0)),
            pl.BlockSpec((1, d), lambda i, j: (0, 0)),
            pl.BlockSpec((d, COL_TILE), lambda i, j: (0, j)),
            pl.BlockSpec((1, COL_TILE), lambda i, j: (0, j)),
            pl.BlockSpec((d, LANES), lambda i, j: (0, 0)),
            pl.BlockSpec((1, LANES), lambda i, j: (0, 0)),
        ],
        out_specs=[
            pl.BlockSpec((ROW_TILE, COL_TILE), lambda i, j: (i, j)),
            pl.BlockSpec((ROW_TILE, LANES), lambda i, j: (i, 0)),
        ],
        out_shape=[
            jax.ShapeDtypeStruct((rows, n_main), BF16),
            jax.ShapeDtypeStruct((rows, LANES), F32),
        ],
        scratch_shapes=[pltpu.VMEM((ROW_TILE, d), BF16)],
        compiler_params=_cparams("parallel", "arbitrary"),
        name="inproj",
    )(hp, mask, nw, wm, bm, wg, bg)


def _split_bf16(x):
    h1 = x.astype(BF16)
    r1 = x - h1.astype(F32)
    h2 = r1.astype(BF16)
    h3 = (r1 - h2.astype(F32)).astype(BF16)
    return h1, h2, h3


def _mlstm_kernel(pad, n_chunks, dk, dv,
                  qf_ref, kf_ref, vf_ref, gf_ref, gtf_ref,
                  qb_ref, kb_ref, vb_ref, gb_ref, gtb_ref,
                  hf_ref, hb_ref, c_sc, n_sc, m_sc):
    c = pl.program_id(1)

    @pl.when(c == 0)
    def _():
        c_sc[...] = jnp.zeros_like(c_sc)
        n_sc[...] = jnp.zeros_like(n_sc)
        m_sc[...] = jnp.zeros_like(m_sc)

    scale = dk ** -0.5
    row = lax.broadcasted_iota(jnp.int32, (CHUNK, CHUNK), 0)
    col = lax.broadcasted_iota(jnp.int32, (CHUNK, CHUNK), 1)
    lower = col <= row
    upper = col >= row
    row1 = lax.broadcasted_iota(jnp.int32, (CHUNK, 1), 0)
    col1 = lax.broadcasted_iota(jnp.int32, (1, CHUNK), 1)

    dirs = (
        (qf_ref, kf_ref, vf_ref, gf_ref, gtf_ref, hf_ref, c, lower, upper),
        (qb_ref, kb_ref, vb_ref, gb_ref, gtb_ref, hb_ref, n_chunks - 1 - c, upper, lower),
    )
    for d, (q_ref, k_ref, v_ref, g_ref, gt_ref, out_ref, chunk, mask, mask_t) in enumerate(dirs):
        base = chunk * CHUNK
        valid_c = (base + row1) >= pad
        valid_r = (base + col1) >= pad
        g0 = 2 * MLSTM_HEADS * d
        gc = g_ref[...]
        gr = gt_ref[...]
        li_c_all = jnp.where(valid_c, gc[:, g0:g0 + MLSTM_HEADS], LOG_ZERO_GATE)
        lf_c_all = jnp.where(valid_c, _log_sigmoid(gc[:, g0 + MLSTM_HEADS:g0 + 2 * MLSTM_HEADS]), 0.0)
        li_r_all = jnp.where(valid_r, gr[g0:g0 + MLSTM_HEADS, :], LOG_ZERO_GATE)
        lf_r_all = jnp.where(valid_r, _log_sigmoid(gr[g0 + MLSTM_HEADS:g0 + 2 * MLSTM_HEADS, :]), 0.0)
        tri = mask.astype(BF16)
        tri_t = mask_t.astype(BF16)
        b_c_all = sum(jnp.dot(tri, part, preferred_element_type=F32) for part in _split_bf16(lf_c_all))
        b_r_all = sum(jnp.dot(part, tri_t, preferred_element_type=F32) for part in _split_bf16(lf_r_all))
        g_all = jnp.sum(lf_c_all, axis=0, keepdims=True)

        for h in range(MLSTM_HEADS):
            idx = d * MLSTM_HEADS + h
            li_c = li_c_all[:, h:h + 1]
            b_c = b_c_all[:, h:h + 1]
            li_r = li_r_all[h:h + 1, :]
            b_r = b_r_all[h:h + 1, :]
            g = g_all[:, h:h + 1]
            m_prev = m_sc[idx, 0:1, 0:1]
            n_prev = n_sc[idx, 0:1, :]
            ct = c_sc[idx]

            qh = q_ref[:, h * dk:(h + 1) * dk]
            kh = k_ref[:, h * dk:(h + 1) * dk]
            vh = v_ref[:, h * dv:(h + 1) * dv]
            k32 = kh.astype(F32)

            a_c = g - b_c + li_c
            m_new = jnp.maximum(g + m_prev, jnp.max(a_c, axis=0, keepdims=True))
            decay = jnp.exp(g + m_prev - m_new)
            w_c = jnp.exp(a_c - m_new)

            inter = b_c + m_prev
            logd = jnp.where(mask, b_c - b_r + li_r, NEG_BIG)
            m_c = jnp.maximum(inter, jnp.max(logd, axis=1, keepdims=True))
            p = jnp.exp(logd - m_c)
            qk = lax.dot_general(qh, kh, (((1,), (1,)), ((), ())), preferred_element_type=F32)
            s = qk * (p * scale)
            w_inter = jnp.exp(inter - m_c) * scale
            qc = jnp.dot(qh, ct.astype(BF16), preferred_element_type=F32)
            num = jnp.dot(s.astype(BF16), vh, preferred_element_type=F32) + w_inter * qc
            qn = jnp.sum(qh.astype(F32) * n_prev, axis=1, keepdims=True)
            den = jnp.sum(s, axis=1, keepdims=True) + w_inter * qn
            out_ref[:, h * dv:(h + 1) * dv] = num / jnp.maximum(jnp.abs(den), jnp.exp(-m_c))

            wv = (w_c * vh.astype(F32)).astype(BF16)
            kt = k32.T.astype(BF16)
            c_sc[idx] = decay * ct + jnp.dot(kt, wv, preferred_element_type=F32)
            n_new = decay * n_prev + jnp.sum(w_c * k32, axis=0, keepdims=True)
            n_sc[idx] = jnp.broadcast_to(n_new, (SUBLANES, dk))
            m_sc[idx] = jnp.broadcast_to(m_new, (SUBLANES, LANES))


def _mlstm(proj, gates, gates_t, batch, lp, pad, dk, dv):
    rows = proj.shape[0]
    n_chunks = lp // CHUNK
    d_qk = MLSTM_HEADS * dk
    d_v = MLSTM_HEADS * dv
    v_blk = (2 * d_qk) // d_v

    def fwd(b, c):
        return b * n_chunks + c

    def bwd(b, c):
        return b * n_chunks + n_chunks - 1 - c

    def specs(pos):
        return [
            pl.BlockSpec((CHUNK, d_qk), lambda b, c: (pos(b, c), 0)),
            pl.BlockSpec((CHUNK, d_qk), lambda b, c: (pos(b, c), 1)),
            pl.BlockSpec((CHUNK, d_v), lambda b, c: (pos(b, c), v_blk)),
            pl.BlockSpec((CHUNK, LANES), lambda b, c: (pos(b, c), 0)),
            pl.BlockSpec((FORGET_GATE_COLS, CHUNK), lambda b, c: (0, pos(b, c))),
        ]

    kern = functools.partial(_mlstm_kernel, pad, n_chunks, dk, dv)
    n_state = 2 * MLSTM_HEADS
    return pl.pallas_call(
        kern,
        grid=(batch, n_chunks),
        in_specs=specs(fwd) + specs(bwd),
        out_specs=[
            pl.BlockSpec((CHUNK, d_v), lambda b, c: (fwd(b, c), 0)),
            pl.BlockSpec((CHUNK, d_v), lambda b, c: (bwd(b, c), 0)),
        ],
        out_shape=[jax.ShapeDtypeStruct((rows, d_v), F32)] * 2,
        scratch_shapes=[
            pltpu.VMEM((n_state, dk, dv), F32),
            pltpu.VMEM((n_state, SUBLANES, dk), F32),
            pltpu.VMEM((n_state, SUBLANES, LANES), F32),
        ],
        compiler_params=_cparams("parallel", "arbitrary"),
        name="mlstm",
    )(proj, proj, proj, gates, gates_t, proj, proj, proj, gates, gates_t)


def _mlstm_out_kernel(dv, hf_ref, hb_ref, o_ref, ga_ref, nw_ref, w_ref, out_ref, y_sc):
    for h in range(MLSTM_HEADS):
        sl = slice(h * dv, (h + 1) * dv)
        hs = hf_ref[:, sl] + hb_ref[:, sl]
        r = lax.rsqrt(jnp.mean(hs * hs, axis=-1, keepdims=True) + EPS)
        y = hs * r * nw_ref[:, sl] * _sigmoid(o_ref[:, sl].astype(F32))
        y_sc[:, sl] = y.astype(BF16)
    ya = jnp.dot(y_sc[...], w_ref[...], preferred_element_type=F32)
    out_ref[...] = _sigmoid(ga_ref[...].astype(F32)) * ya


def _mlstm_out(hf, hb, proj, nw, w, dv, o_blk, ga_blk):
    rows, d_v = hf.shape
    d = w.shape[1]
    return pl.pallas_call(
        functools.partial(_mlstm_out_kernel, dv),
        grid=(rows // ROW_TILE,),
        in_specs=[
            pl.BlockSpec((ROW_TILE, d_v), lambda i: (i, 0)),
            pl.BlockSpec((ROW_TILE, d_v), lambda i: (i, 0)),
            pl.BlockSpec((ROW_TILE, d_v), lambda i: (i, o_blk)),
            pl.BlockSpec((ROW_TILE, d), lambda i: (i, ga_blk)),
            pl.BlockSpec((1, d_v), lambda i: (0, 0)),
            pl.BlockSpec((d_v, d), lambda i: (0, 0)),
        ],
        out_specs=pl.BlockSpec((ROW_TILE, d), lambda i: (i, 0)),
        out_shape=jax.ShapeDtypeStruct((rows, d), F32),
        scratch_shapes=[pltpu.VMEM((ROW_TILE, d_v), BF16)],
        compiler_params=_cparams("parallel"),
        name="mlstm_out",
    )(hf, hb, proj, proj, nw, w)


CONV_ROW_BLOCK = 256
CONV_COL_BLOCK = 256


def _conv_kernel(a_ref, g_ref, ap_ref, gp_ref, an_ref, gn_ref, cw_ref, cb_ref, lnw_ref, lnb_ref,
                 w_ref, bo_ref, gb_ref, ya_ref, out_ref, c_sc, conv_sc):
    i = pl.program_id(0)
    last = pl.num_programs(0) - 1

    def glu(a, g):
        return a.astype(F32) * _sigmoid(g.astype(F32))

    c_sc[HALO:HALO + ROW_TILE, :] = glu(a_ref[...], g_ref[...])
    c_sc[0:HALO, :] = jnp.where(i > 0, glu(ap_ref[...], gp_ref[...]), 0.0)
    c_sc[HALO + ROW_TILE:, :] = jnp.where(i < last, glu(an_ref[...], gn_ref[...]), 0.0)

    d = a_ref.shape[1]
    first = HALO - CONV_HALF
    span = CONV_ROW_BLOCK + 2 * HALO
    for rb in range(ROW_TILE // CONV_ROW_BLOCK):
        r0 = rb * CONV_ROW_BLOCK
        for cb in range(d // CONV_COL_BLOCK):
            cs = slice(cb * CONV_COL_BLOCK, (cb + 1) * CONV_COL_BLOCK)
            x = c_sc[r0:r0 + span, cs]
            acc = jnp.broadcast_to(cb_ref[:, cs], (CONV_ROW_BLOCK, CONV_COL_BLOCK))
            for sh in range(SUBLANES):
                xs = x if sh == 0 else pltpu.roll(x, span - sh, 0)
                for al in range((CONV_WIDTH + first) // SUBLANES + 1):
                    tap = al * SUBLANES + sh - first
                    if 0 <= tap < CONV_WIDTH:
                        acc = acc + cw_ref[tap:tap + 1, cs] * xs[al * SUBLANES:al * SUBLANES + CONV_ROW_BLOCK]
            conv_sc[r0:r0 + CONV_ROW_BLOCK, cs] = acc

    x = conv_sc[...]
    mu = jnp.mean(x, axis=-1, keepdims=True)
    xc = x - mu
    var = jnp.mean(xc * xc, axis=-1, keepdims=True)
    y = xc * lax.rsqrt(var + EPS) * lnw_ref[...] + lnb_ref[...]
    y = y * _sigmoid(y)
    yb = jnp.dot(y.astype(BF16), w_ref[...], preferred_element_type=F32) + bo_ref[...]
    out_ref[...] = ya_ref[...] + _sigmoid(gb_ref[...].astype(F32)) * yb


def _conv_merge(proj, ya, cw, cb, lnw, lnb, w, bo, a_blk, g_blk, gb_blk):
    rows = proj.shape[0]
    d = w.shape[0]
    n_tiles = rows // ROW_TILE
    per = ROW_TILE // HALO
    n_halo = rows // HALO
    a_col = a_blk
    g_col = g_blk

    def prev(i):
        return jnp.maximum(i * per - 1, 0)

    def nxt(i):
        return jnp.minimum((i + 1) * per, n_halo - 1)

    return pl.pallas_call(
        _conv_kernel,
        grid=(n_tiles,),
        in_specs=[
            pl.BlockSpec((ROW_TILE, d), lambda i: (i, a_col)),
            pl.BlockSpec((ROW_TILE, d), lambda i: (i, g_col)),
            pl.BlockSpec((HALO, d), lambda i: (prev(i), a_col)),
            pl.BlockSpec((HALO, d), lambda i: (prev(i), g_col)),
            pl.BlockSpec((HALO, d), lambda i: (nxt(i), a_col)),
            pl.BlockSpec((HALO, d), lambda i: (nxt(i), g_col)),
            pl.BlockSpec((CONV_WIDTH, d), lambda i: (0, 0)),
            pl.BlockSpec((1, d), lambda i: (0, 0)),
            pl.BlockSpec((1, d), lambda i: (0, 0)),
            pl.BlockSpec((1, d), lambda i: (0, 0)),
            pl.BlockSpec((d, d), lambda i: (0, 0)),
            pl.BlockSpec((1, d), lambda i: (0, 0)),
            pl.BlockSpec((ROW_TILE, d), lambda i: (i, gb_blk)),
            pl.BlockSpec((ROW_TILE, d), lambda i: (i, 0)),
        ],
        out_specs=pl.BlockSpec((ROW_TILE, d), lambda i: (i, 0)),
        out_shape=jax.ShapeDtypeStruct((rows, d), F32),
        scratch_shapes=[
            pltpu.VMEM((ROW_TILE + 2 * HALO, d), F32),
            pltpu.VMEM((ROW_TILE, d), F32),
        ],
        compiler_params=_cparams("parallel"),
        name="conv_merge",
    )(proj, proj, proj, proj, proj, proj, cw, cb, lnw, lnb, w, bo, proj, ya)


def _wo_kernel(m_ref, h_ref, w_ref, nw_ref, h2_ref, ut_ref):
    h2 = h_ref[...] + jnp.dot(m_ref[...].astype(BF16), w_ref[...], preferred_element_type=F32)
    h2_ref[...] = h2
    r = lax.rsqrt(jnp.mean(h2 * h2, axis=-1, keepdims=True) + EPS)
    u = h2 * r * nw_ref[...]
    ut_ref[...] = u.T.astype(BF16)


def _wo_residual(merged, hp, w, nw):
    rows, d = hp.shape
    return pl.pallas_call(
        _wo_kernel,
        grid=(rows // ROW_TILE,),
        in_specs=[
            pl.BlockSpec((ROW_TILE, d), lambda i: (i, 0)),
            pl.BlockSpec((ROW_TILE, d), lambda i: (i, 0)),
            pl.BlockSpec((d, d), lambda i: (0, 0)),
            pl.BlockSpec((1, d), lambda i: (0, 0)),
        ],
        out_specs=[
            pl.BlockSpec((ROW_TILE, d), lambda i: (i, 0)),
            pl.BlockSpec((d, ROW_TILE), lambda i: (0, i)),
        ],
        out_shape=[
            jax.ShapeDtypeStruct((rows, d), F32),
            jax.ShapeDtypeStruct((d, rows), BF16),
        ],
        compiler_params=_cparams("parallel"),
        name="wo_residual",
    )(merged, hp, w, nw)


def _top_values(x, k):
    vals = []
    for _ in range(k):
        mx = jnp.max(x, axis=0, keepdims=True)
        vals.append(mx)
        x = jnp.where(x == mx, -jnp.inf, x)
    return jnp.concatenate(vals, axis=0)


def _odd_even_merge_sort_pairs(n):
    pairs = []

    def merge(lo, size, r):
        step = 2 * r
        if step < size:
            merge(lo, size, step)
            merge(lo + r, size, step)
            pairs.extend((i, i + r) for i in range(lo + r, lo + size - r, step))
        else:
            pairs.append((lo, lo + r))

    def sort(lo, size):
        if size > 1:
            sort(lo, size // 2)
            sort(lo + size // 2, size // 2)
            merge(lo, size, 1)

    sort(0, n)
    return pairs


def _compare_exchange(v, i, j):
    v[i], v[j] = jnp.maximum(v[i], v[j]), jnp.minimum(v[i], v[j])


def _top_values_network(x, k):
    assert x.shape[0] == SUBLANES * k and k & (k - 1) == 0
    v = [x[r * SUBLANES:(r + 1) * SUBLANES, :] for r in range(k)]
    for i, j in _odd_even_merge_sort_pairs(k):
        _compare_exchange(v, i, j)
    shift = SUBLANES // 2
    while shift:
        v = [jnp.maximum(v[i], pltpu.roll(v[k - 1 - i], shift, 0)) for i in range(k)]
        stride = k // 2
        while stride:
            for i in range(k):
                if i & stride == 0:
                    _compare_exchange(v, i, i + stride)
            stride //= 2
        shift //= 2
    return jnp.concatenate([b[0:1, :] for b in v], axis=0)


def _peer_route_kernel(n_keys, ut_ref, wq_ref, keys_ref, t1_ref, e1_ref, s2_ref, e2_ref):
    k = PEER_TOPK
    ut = ut_ref[...]
    tokens = ut.shape[1]
    q_all = jnp.dot(wq_ref[...], ut, preferred_element_type=F32).astype(BF16)
    for h in range(PEER_HEADS):
        tops = []
        scores = []
        for p in range(2):
            r0 = (2 * h + p) * n_keys
            s = jnp.dot(keys_ref[p], q_all[r0:r0 + n_keys, :], preferred_element_type=F32)
            scores.append(s)
            tops.append(_top_values_network(s, k))
        a1, a2 = tops
        n_y = [k // (x + 1) for x in range(k)]
        cand_rows = [a1[x:x + 1, :] + a2[0:n_y[x], :] for x in range(k)]
        fill = (-sum(n_y)) % SUBLANES
        if fill:
            cand_rows.append(jnp.full((fill, tokens), -jnp.inf, F32))
        top = _top_values(jnp.concatenate(cand_rows, axis=0), k)
        cmax = top[0:1, :]
        theta = top[k - 1:k, :]
        z = jnp.sum(jnp.exp(top - cmax), axis=0, keepdims=True)
        tau = jnp.full(scores[0].shape, jnp.inf, F32)
        for x in range(k):
            tau_x = jnp.min(jnp.where(cand_rows[x] >= theta, a2[0:n_y[x], :], jnp.inf), axis=0, keepdims=True)
            tau = jnp.where(scores[0] == a1[x:x + 1, :], tau_x, tau)
        t1_ref[h] = tau
        s2_ref[h] = scores[1]
        e1_ref[h] = jnp.exp(scores[0] - a1[0:1, :]) / z
        e2_ref[h] = jnp.exp(scores[1] - a2[0:1, :])


def _peer_route(ut, wq_t, keys):
    d, rows = ut.shape
    n_keys = keys.shape[1]
    tm = PEER_TOKEN_TILE
    big = pl.BlockSpec((PEER_HEADS, n_keys, tm), lambda t: (0, 0, t))
    big_shape = jax.ShapeDtypeStruct((PEER_HEADS, n_keys, rows), F32)
    return pl.pallas_call(
        functools.partial(_peer_route_kernel, n_keys),
        grid=(rows // tm,),
        in_specs=[
            pl.BlockSpec((d, tm), lambda t: (0, t)),
            pl.BlockSpec(wq_t.shape, lambda t: (0, 0)),
            pl.BlockSpec(keys.shape, lambda t: (0, 0, 0)),
        ],
        out_specs=[big, big, big, big],
        out_shape=[big_shape, big_shape, big_shape, big_shape],
        compiler_params=_cparams("parallel"),
        name="peer_route",
    )(ut, wq_t, keys)


PEER_GATE_ROWS = 128


def _peer_part(n_keys, part, tile, ut_ref, u_ref, vt_ref, gate_refs, out_ref, z_sc, act_new, act_old):
    t1_ref, e1_ref, s2_ref, e2_ref = gate_refs
    te, tm = PEER_EXPERT_TILE, PEER_TOKEN_TILE
    d = ut_ref.shape[0]
    gr = PEER_GATE_ROWS
    per = te // n_keys
    out_rows = d // per
    for il in range(per):
        i = tile * per + il
        c0 = il * n_keys
        z_sc[c0:c0 + n_keys, :] = jnp.dot(u_ref[part * te + c0:part * te + c0 + n_keys, :], ut_ref[...],
                                          preferred_element_type=F32)
        for jb in range(n_keys // gr):
            js = slice(jb * gr, (jb + 1) * gr)
            rows = slice(c0 + jb * gr, c0 + (jb + 1) * gr)
            for lc in range(tm // LANES):
                cols = slice(lc * LANES, (lc + 1) * LANES)
                gate = None
                for h in range(PEER_HEADS):
                    sel = jnp.where(s2_ref[h, js, cols] >= t1_ref[h, pl.ds(i, 1), :][:, cols],
                                    e2_ref[h, js, cols], 0.0)
                    term = sel * e1_ref[h, pl.ds(i, 1), :][:, cols]
                    gate = term if gate is None else gate + term
                act_new[rows, cols] = (gate * _gelu_tanh(z_sc[rows, cols])).astype(BF16)
        o0 = il * out_rows
        out_ref[o0:o0 + out_rows, :] += jnp.dot(vt_ref[o0:o0 + out_rows, :], act_old[...],
                                                preferred_element_type=F32)


def _peer_main_kernel(n_keys, n_pairs, ut_ref, u_ref, vta_ref, vtb_ref, vtl_ref, t1_ref, e1_ref, s2_ref,
                      e2_ref, out_ref, z_sc, act0_sc, act1_sc):
    s = pl.program_id(1)

    @pl.when(s == 0)
    def _():
        out_ref[...] = jnp.zeros_like(out_ref)
        act1_sc[...] = jnp.zeros_like(act1_sc)

    gate_refs = (t1_ref, e1_ref, s2_ref, e2_ref)
    _peer_part(n_keys, 0, 2 * s, ut_ref, u_ref, vta_ref, gate_refs, out_ref, z_sc, act0_sc, act1_sc)
    _peer_part(n_keys, 1, 2 * s + 1, ut_ref, u_ref, vtb_ref, gate_refs, out_ref, z_sc, act1_sc, act0_sc)

    @pl.when(s == n_pairs - 1)
    def _():
        out_ref[...] += jnp.dot(vtl_ref[...], act1_sc[...], preferred_element_type=F32)


def _peer_main(ut, u_w, v_t, t1, e1, s2, e2):
    d, rows = ut.shape
    n_exp = u_w.shape[0]
    n_keys = t1.shape[1]
    tm, te = PEER_TOKEN_TILE, PEER_EXPERT_TILE
    n_pairs = n_exp // (2 * te)
    big = pl.BlockSpec((PEER_HEADS, n_keys, tm), lambda t, s: (0, 0, t))
    return pl.pallas_call(
        functools.partial(_peer_main_kernel, n_keys, n_pairs),
        grid=(rows // tm, n_pairs),
        in_specs=[
            pl.BlockSpec((d, tm), lambda t, s: (0, t)),
            pl.BlockSpec((2 * te, d), lambda t, s: (s, 0)),
            pl.BlockSpec((d, te), lambda t, s: (0, jnp.maximum(2 * s - 1, 0))),
            pl.BlockSpec((d, te), lambda t, s: (0, 2 * s)),
            pl.BlockSpec((d, te), lambda t, s: (0, 2 * n_pairs - 1)),
            big, big, big, big,
        ],
        out_specs=pl.BlockSpec((d, tm), lambda t, s: (0, t)),
        out_shape=jax.ShapeDtypeStruct((d, rows), F32),
        scratch_shapes=[
            pltpu.VMEM((te, tm), F32),
            pltpu.VMEM((te, tm), BF16),
            pltpu.VMEM((te, tm), BF16),
        ],
        compiler_params=_cparams("parallel", "arbitrary"),
        name="peer_main",
    )(ut, u_w, v_t, v_t, v_t, t1, e1, s2, e2)


def _final_kernel(pt_ref, h2_ref, nw_ref, out_ref):
    h3 = h2_ref[...] + pt_ref[...].T
    r = lax.rsqrt(jnp.mean(h3 * h3, axis=-1, keepdims=True) + EPS)
    out_ref[0] = h3 * r * nw_ref[...]


def _final(peer_t, h2, nw, batch, seq, lp):
    d = h2.shape[1]
    tiles_per_batch = lp // OUT_TILE
    skip = tiles_per_batch - seq // OUT_TILE

    def pos(b, i):
        return b * tiles_per_batch + skip + i

    return pl.pallas_call(
        _final_kernel,
        grid=(batch, seq // OUT_TILE),
        in_specs=[
            pl.BlockSpec((d, OUT_TILE), lambda b, i: (0, pos(b, i))),
            pl.BlockSpec((OUT_TILE, d), lambda b, i: (pos(b, i), 0)),
            pl.BlockSpec((1, d), lambda b, i: (0, 0)),
        ],
        out_specs=pl.BlockSpec((1, OUT_TILE, d), lambda b, i: (b, i, 0)),
        out_shape=jax.ShapeDtypeStruct((batch, seq, d), F32),
        compiler_params=_cparams("parallel", "parallel"),
        name="final_norm",
    )(peer_t, h2, nw)


def _layer(hp, mask, batch, lp, pad, norm1_w, w_in, b_in, mlstm_norm_w, w_mlstm_out, conv_w, conv_b,
           conv_ln_w, conv_ln_b, w_conv_out, b_conv_out, w_o, norm2_w, peer_w_q, peer_sub_keys,
           peer_u, peer_v):
    d = hp.shape[1]
    d_v = w_mlstm_out.shape[0]
    dv = d_v // MLSTM_HEADS
    d_conv = w_conv_out.shape[0]
    d_qk = (w_in.shape[1] - 2 * d_v - FORGET_GATE_COLS - 2 * d_conv - 2 * d) // 2
    dk = d_qk // MLSTM_HEADS
    assert d_v == d and d_conv == d and 2 * d_qk == d_v, "column-block indexing assumes these widths"

    g0 = 2 * d_qk + 2 * d_v
    g1 = g0 + FORGET_GATE_COLS
    wm = jnp.concatenate([w_in[:, :g0], w_in[:, g1:]], axis=1).astype(BF16)
    bm = jnp.concatenate([b_in[:g0], b_in[g1:]])[None, :]
    wg = jnp.pad(w_in[:, g0:g1], ((0, 0), (0, LANES - FORGET_GATE_COLS))).astype(BF16)
    bg = jnp.pad(b_in[g0:g1], (0, LANES - FORGET_GATE_COLS))[None, :]
    o_blk, a_blk, g_blk, ga_blk, gb_blk = 2, 3, 4, 5, 6

    proj, gates = _inproj(hp, mask, norm1_w[None, :], wm, bm, wg, bg)
    gates_t = gates[:, :FORGET_GATE_COLS].T
    hf, hb = _mlstm(proj, gates, gates_t, batch, lp, pad, dk, dv)
    ya = _mlstm_out(hf, hb, proj, mlstm_norm_w[None, :], w_mlstm_out.astype(BF16), dv, o_blk, ga_blk)
    merged = _conv_merge(proj, ya, conv_w, conv_b[None, :], conv_ln_w[None, :], conv_ln_b[None, :],
                         w_conv_out.astype(BF16), b_conv_out[None, :], a_blk, g_blk, gb_blk)
    h2, ut = _wo_residual(merged, hp, w_o.astype(BF16), norm2_w[None, :])
    t1, e1, s2, e2 = _peer_route(ut, peer_w_q.T.astype(BF16), peer_sub_keys.astype(BF16))
    peer_t = _peer_main(ut, peer_u.astype(BF16), peer_v.T.astype(BF16), t1, e1, s2, e2)
    return h2, peer_t


def kernel(x, meta_tokens, norm1_w, w_in, b_in, mlstm_norm_w, w_mlstm_out, conv_w, conv_b, conv_ln_w,
           conv_ln_b, w_conv_out, b_conv_out, w_o, norm2_w, peer_w_q, peer_sub_keys, peer_u, peer_v,
           final_norm_w):
    batch, seq, d = x.shape
    depth = norm1_w.shape[0]
    assert depth == 1, "the fused final norm assumes a single layer"
    assert seq % OUT_TILE == 0
    seq_len = N_META + seq
    pad = HALO + (-(seq_len + HALO)) % CHUNK
    lp = pad + seq_len
    rows = batch * lp
    assert rows % ROW_TILE == 0 and rows % PEER_TOKEN_TILE == 0

    meta = jnp.broadcast_to(meta_tokens[None].astype(x.dtype), (batch, N_META, d))
    hp = jnp.concatenate([jnp.zeros((batch, pad, d), x.dtype), meta, x], axis=1).reshape(rows, d)
    mask = jnp.tile((jnp.arange(lp) >= pad).astype(F32), batch)[:, None]

    h2, peer_t = _layer(hp, mask, batch, lp, pad, norm1_w[0], w_in[0], b_in[0], mlstm_norm_w[0],
                        w_mlstm_out[0], conv_w[0], conv_b[0], conv_ln_w[0], conv_ln_b[0],
                        w_conv_out[0], b_conv_out[0], w_o[0], norm2_w[0], peer_w_q[0],
                        peer_sub_keys[0], peer_u[0], peer_v[0])
    return _final(peer_t, h2, final_norm_w[None, :], batch, seq, lp)
```
